```python
import jax, jax.numpy as jnp
from jax import lax
import numpy as np

D_MODEL = 2048
BATCH = 1
SEQ = 16384
DEPTH = 2

CHUNK = 64
D_MIX = D_MODEL
D_CONV = D_MIX // 2
D_GLA = D_MIX - D_CONV
N_GLA_HEADS = 4
HEAD_V = D_GLA // N_GLA_HEADS
HEAD_K = HEAD_V // 2
D_QK = N_GLA_HEADS * HEAD_K
GATE_RANK = 16
GATE_TAU = 16.0
CONV_WIDTH = 3
D_FF = ((int(8 * D_MODEL / 3) + 255) // 256) * 256
FFN_RES_SCALE = 0.5
EPS = 1e-6
IN_COLS = 3 * D_CONV + 2 * D_QK + 2 * D_GLA + GATE_RANK

kernel_name = "hymba_conv_gla_macaron_trunk"


def _rmsnorm(x, g):
    xf = x.astype(jnp.float32)
    y = xf * lax.rsqrt(jnp.mean(xf * xf, axis=-1, keepdims=True) + EPS)
    return (y * g.astype(jnp.float32)).astype(x.dtype)


def _swiglu(h, w_gate, w_up, w_down):
    return (jax.nn.silu(h @ w_gate) * (h @ w_up)) @ w_down


def _causal_dwconv(u, w):
    c = u.shape[-1]
    return lax.conv_general_dilated(
        u, w[:, None, :].astype(u.dtype), window_strides=(1,),
        padding=[(CONV_WIDTH - 1, 0)], dimension_numbers=("NWC", "WIO", "NWC"),
        feature_group_count=c)


def _gla_chunk_causal(q, k, v, log_a):
    b, t, h, _ = q.shape
    nc = t // CHUNK

    def to_chunks(a):
        return a.reshape(b, nc, CHUNK, h, a.shape[-1]).transpose(1, 0, 3, 2, 4).astype(jnp.float32)

    qc, kc, vc, ac = to_chunks(q), to_chunks(k), to_chunks(v), to_chunks(log_a)

    def step(state, inp):
        qi, ki, vi, ai = inp
        bcum = jnp.cumsum(ai, axis=2)
        o_inter = jnp.einsum("bhtk,bhkv->bhtv", qi * jnp.exp(bcum), state)
        decay = jnp.exp(-jnp.abs(bcum[:, :, :, None, :] - bcum[:, :, None, :, :]))
        scores = jnp.einsum("bhtk,bhsk,bhtsk->bhts", qi, ki, decay)
        o_intra = jnp.einsum("bhts,bhsv->bhtv", scores, vi)
        blast = bcum[:, :, -1:, :]
        new_state = jnp.exp(blast[:, :, 0, :])[..., None] * state + jnp.einsum(
            "bhsk,bhsv->bhkv", ki * jnp.exp(blast - bcum), vi)
        return new_state, o_inter + o_intra

    s0 = jnp.zeros((b, h, q.shape[-1], v.shape[-1]), jnp.float32)
    _, out = lax.scan(step, s0, (qc, kc, vc, ac))
    return out.transpose(1, 0, 3, 2, 4).reshape(b, t, h, v.shape[-1])


def _token_mixing(h, w_in, conv_w, gate_w2, gate_b, gla_norm, w_out):
    b, t, _ = h.shape
    proj = h @ w_in
    sizes = (D_CONV, D_CONV, D_CONV, D_QK, D_QK, D_GLA, D_GLA, GATE_RANK)
    cuts = [int(c) for c in np.cumsum(sizes)[:-1]]
    xv, gb, gc, q, k, v, g, zl = jnp.split(proj, cuts, axis=-1)

    y_conv = gb * _causal_dwconv(gc * xv, conv_w)

    log_a = jax.nn.log_sigmoid((zl @ gate_w2 + gate_b).astype(jnp.float32)) / GATE_TAU
    qh = q.reshape(b, t, N_GLA_HEADS, HEAD_K) * (HEAD_K ** -0.5)
    kh = k.reshape(b, t, N_GLA_HEADS, HEAD_K)
    vh = v.reshape(b, t, N_GLA_HEADS, HEAD_V)
    o = _gla_chunk_causal(qh, kh, vh, log_a.reshape(b, t, N_GLA_HEADS, HEAD_K))
    o = _rmsnorm(o, gla_norm).astype(h.dtype).reshape(b, t, D_GLA)
    y_gla = o * jax.nn.silu(g)

    return jnp.concatenate([y_conv, y_gla], axis=-1) @ w_out


def setup_inputs(seed: int = 0) -> dict:
    key = jax.random.key(seed)
    ks = jax.random.split(key, 20)
    f32 = jnp.float32

    def nrm(k, shape, fan_in):
        return jax.random.normal(k, shape, f32) * (fan_in ** -0.5)

    def gain(k, shape):
        return 1.0 + 0.02 * jax.random.normal(k, shape, f32)

    return {
        "x": jax.random.normal(ks[0], (BATCH, SEQ, D_MODEL), f32),
        "ffn1_norm": gain(ks[1], (DEPTH, D_MODEL)),
        "ffn1_w_gate": nrm(ks[2], (DEPTH, D_MODEL, D_FF), D_MODEL),
        "ffn1_w_up": nrm(ks[3], (DEPTH, D_MODEL, D_FF), D_MODEL),
        "ffn1_w_down": nrm(ks[4], (DEPTH, D_FF, D_MODEL), D_FF),
        "mix_norm": gain(ks[5], (DEPTH, D_MODEL)),
        "w_in": nrm(ks[6], (DEPTH, D_MODEL, IN_COLS), D_MODEL),
        "conv_w": nrm(ks[7], (DEPTH, CONV_WIDTH, D_CONV), CONV_WIDTH),
        "gate_w2": nrm(ks[8], (DEPTH, GATE_RANK, D_QK), GATE_RANK),
        "gate_b": 0.01 * jax.random.normal(ks[9], (DEPTH, D_QK), f32),
        "gla_norm": gain(ks[10], (DEPTH, HEAD_V)),
        "w_out": nrm(ks[11], (DEPTH, D_MIX, D_MODEL), D_MIX),
        "ffn2_norm": gain(ks[12], (DEPTH, D_MODEL)),
        "ffn2_w_gate": nrm(ks[13], (DEPTH, D_MODEL, D_FF), D_MODEL),
        "ffn2_w_up": nrm(ks[14], (DEPTH, D_MODEL, D_FF), D_MODEL),
        "ffn2_w_down": nrm(ks[15], (DEPTH, D_FF, D_MODEL), D_FF),
        "final_norm": gain(ks[16], (D_MODEL,)),
    }


def reference(x, ffn1_norm, ffn1_w_gate, ffn1_w_up, ffn1_w_down, mix_norm, w_in, conv_w,
              gate_w2, gate_b, gla_norm, w_out, ffn2_norm, ffn2_w_gate, ffn2_w_up,
              ffn2_w_down, final_norm):
    for l in range(DEPTH):
        x = x + FFN_RES_SCALE * _swiglu(_rmsnorm(x, ffn1_norm[l]),
                                        ffn1_w_gate[l], ffn1_w_up[l], ffn1_w_down[l])
        x = x + _token_mixing(_rmsnorm(x, mix_norm[l]), w_in[l], conv_w[l], gate_w2[l],
                              gate_b[l], gla_norm[l], w_out[l])
        x = x + FFN_RES_SCALE * _swiglu(_rmsnorm(x, ffn2_norm[l]),
                                        ffn2_w_gate[l], ffn2_w_up[l], ffn2_w_down[l])
    return _rmsnorm(x, final_norm)
```

```python
import functools

import jax
import jax.numpy as jnp
import numpy as np
from jax import lax
from jax.experimental import pallas as pl
from jax.experimental.pallas import tpu as pltpu

D_MODEL = 2048
CHUNK = 64
D_CONV = 1024
D_GLA = 1024
N_HEADS = 4
HEAD_V = 256
HEAD_K = 128
D_QK = 512
GATE_RANK = 16
GATE_TAU = 16.0
CONV_WIDTH = 3
D_FF = 5632
EPS = 1e-6
MAIN_COLS = 3 * D_CONV + 2 * D_QK + 2 * D_GLA
LANE = 128
SUBLANE = 8
N_LEVELS = 6
VMEM_LIMIT = 56 * 1024 * 1024

F32 = jnp.float32
BF16 = jnp.bfloat16


def _rms(x, w):
    ms = jnp.mean(x * x, axis=-1, keepdims=True)
    return x * lax.rsqrt(ms + EPS) * w


def _sigmoid(x):
    return 1.0 / (1.0 + jnp.exp(-x))


def _ffn_body(x_ref, nw_ref, wg_ref, wu_ref, wd_ref, fw_ref, o_ref, h_ref, *, n_f, final):
    j = pl.program_id(1)

    @pl.when(j == 0)
    def _():
        h_ref[...] = _rms(x_ref[...], nw_ref[...]).astype(BF16)
        o_ref[...] = jnp.zeros_like(o_ref)

    h = h_ref[...]
    g = jnp.dot(h, wg_ref[...], preferred_element_type=F32)
    u = jnp.dot(h, wu_ref[...], preferred_element_type=F32)
    a = (g * _sigmoid(g) * u).astype(BF16)
    o_ref[...] += jnp.dot(a, wd_ref[...], preferred_element_type=F32)

    @pl.when(j == n_f - 1)
    def _():
        r = x_ref[...] + 0.5 * o_ref[...]
        if final:
            r = _rms(r, fw_ref[...])
        o_ref[...] = r


def _ffn(x, nw, wg, wu, wd, fw, *, final, tm=512, tf=512):
    t = x.shape[0]
    n_f = D_FF // tf
    return pl.pallas_call(
        functools.partial(_ffn_body, n_f=n_f, final=final),
        grid=(t // tm, n_f),
        in_specs=[
            pl.BlockSpec((tm, D_MODEL), lambda i, j: (i, 0)),
            pl.BlockSpec((1, D_MODEL), lambda i, j: (0, 0)),
            pl.BlockSpec((D_MODEL, tf), lambda i, j: (0, j)),
            pl.BlockSpec((D_MODEL, tf), lambda i, j: (0, j)),
            pl.BlockSpec((tf, D_MODEL), lambda i, j: (j, 0)),
            pl.BlockSpec((1, D_MODEL), lambda i, j: (0, 0)),
        ],
        out_specs=pl.BlockSpec((tm, D_MODEL), lambda i, j: (i, 0)),
        out_shape=jax.ShapeDtypeStruct((t, D_MODEL), F32),
        scratch_shapes=[pltpu.VMEM((tm, D_MODEL), BF16)],
        compiler_params=pltpu.CompilerParams(
            dimension_semantics=("arbitrary", "arbitrary"), vmem_limit_bytes=VMEM_LIMIT),
        name="ffn_final" if final else "ffn",
    )(x, nw, wg, wu, wd, fw)


def _inproj_body(x_ref, nw_ref, w_ref, wz_ref, o_ref, z_ref, h_ref):
    j = pl.program_id(1)

    @pl.when(j == 0)
    def _():
        h = _rms(x_ref[...], nw_ref[...]).astype(BF16)
        h_ref[...] = h
        z_ref[...] = jnp.dot(h, wz_ref[...], preferred_element_type=F32)

    o_ref[...] = jnp.dot(h_ref[...], w_ref[...], preferred_element_type=F32)


def _inproj(x, nw, w_main, w_z, *, tm=512, tn=1024):
    t = x.shape[0]
    return pl.pallas_call(
        _inproj_body,
        grid=(t // tm, MAIN_COLS // tn),
        in_specs=[
            pl.BlockSpec((tm, D_MODEL), lambda i, j: (i, 0)),
            pl.BlockSpec((1, D_MODEL), lambda i, j: (0, 0)),
            pl.BlockSpec((D_MODEL, tn), lambda i, j: (0, j)),
            pl.BlockSpec((D_MODEL, LANE), lambda i, j: (0, 0)),
        ],
        out_specs=[
            pl.BlockSpec((tm, tn), lambda i, j: (i, j)),
            pl.BlockSpec((tm, LANE), lambda i, j: (i, 0)),
        ],
        out_shape=[
            jax.ShapeDtypeStruct((t, MAIN_COLS), F32),
            jax.ShapeDtypeStruct((t, LANE), F32),
        ],
        scratch_shapes=[pltpu.VMEM((tm, D_MODEL), BF16)],
        compiler_params=pltpu.CompilerParams(
            dimension_semantics=("arbitrary", "arbitrary"), vmem_limit_bytes=VMEM_LIMIT),
        name="in_proj",
    )(x, nw, w_main, w_z)


def _decay_sum_matrix():
    t = np.arange(CHUNK)[:, None]
    i = np.arange(CHUNK)[None, :]
    blocks = [(i <= t), (i > t)]
    for l in range(N_LEVELS):
        t0 = ((t >> (l + 1)) << (l + 1)) + (1 << l)
        right = t >= t0
        blocks.append(np.where(right, (i > t0) & (i <= t), (i > t) & (i <= t0)))
    return np.concatenate(blocks, axis=0).astype(np.float32)


def _mix_body(p_ref, z_ref, x_ref, cw_ref, gw2_ref, gb_ref, gn_ref, dm_ref, wo_ref, o_ref,
              s_ref, ubuf_ref, la_ref, y_ref, *, tm):
    i = pl.program_id(0)

    @pl.when(i == 0)
    def _():
        s_ref[...] = jnp.zeros_like(s_ref)
        ubuf_ref[0:SUBLANE, :] = jnp.zeros((SUBLANE, D_CONV), F32)

    u = p_ref[:, 2 * D_CONV:3 * D_CONV] * p_ref[:, 0:D_CONV]
    ubuf_ref[SUBLANE:SUBLANE + tm, :] = u
    cw = cw_ref[...]
    conv = (cw[0:1, :] * ubuf_ref[SUBLANE - 2:SUBLANE - 2 + tm, :]
            + cw[1:2, :] * ubuf_ref[SUBLANE - 1:SUBLANE - 1 + tm, :]
            + cw[2:3, :] * u)
    y_ref[:, 0:D_CONV] = (p_ref[:, D_CONV:2 * D_CONV] * conv).astype(BF16)
    ubuf_ref[0:SUBLANE, :] = ubuf_ref[tm:tm + SUBLANE, :]

    zz = jnp.dot(z_ref[...].astype(BF16), gw2_ref[...], preferred_element_type=F32) + gb_ref[...]
    la_ref[...] = (jnp.minimum(zz, 0.0) - jnp.log(1.0 + jnp.exp(-jnp.abs(zz)))) * (1.0 / GATE_TAU)

    row = lax.broadcasted_iota(jnp.int32, (CHUNK, CHUNK), 0)
    col = lax.broadcasted_iota(jnp.int32, (CHUNK, CHUNK), 1)
    rc = row ^ col
    q_scale = HEAD_K ** -0.5
    off_q = 3 * D_CONV
    off_k = off_q + D_QK
    off_v = off_k + D_QK
    off_g = off_v + D_GLA

    def chunk_step(c, carry):
        r0 = pl.multiple_of(c * CHUNK, CHUNK)
        rows = pl.ds(r0, CHUNK)
        la = la_ref[rows, :]
        la_hi = la.astype(BF16)
        r1 = la - la_hi.astype(F32)
        la_mid = r1.astype(BF16)
        la_lo = (r1 - la_mid.astype(F32)).astype(BF16)
        dm = dm_ref[...]
        sums = (jnp.dot(dm, la_hi, preferred_element_type=F32)
                + jnp.dot(dm, la_mid, preferred_element_type=F32)
                + jnp.dot(dm, la_lo, preferred_element_type=F32))

        for hd in range(N_HEADS):
            ks = slice(hd * HEAD_K, (hd + 1) * HEAD_K)
            vs = slice(hd * HEAD_V, (hd + 1) * HEAD_V)
            qh = p_ref[rows, off_q + hd * HEAD_K:off_q + (hd + 1) * HEAD_K] * q_scale
            kh = p_ref[rows, off_k + hd * HEAD_K:off_k + (hd + 1) * HEAD_K]
            vh = p_ref[rows, off_v + hd * HEAD_V:off_v + (hd + 1) * HEAD_V].astype(BF16)
            b = sums[0:CHUNK, ks]
            suffix = sums[CHUNK:2 * CHUNK, ks]

            scores = jnp.where(
                rc == 0,
                lax.dot_general(qh.astype(BF16), kh.astype(BF16), (((1,), (1,)), ((), ())),
                                preferred_element_type=F32),
                0.0)
            for l in range(N_LEVELS):
                gl = jnp.exp(sums[(2 + l) * CHUNK:(3 + l) * CHUNK, ks])
                pl_ = lax.dot_general((qh * gl).astype(BF16), (kh * gl).astype(BF16),
                                      (((1,), (1,)), ((), ())), preferred_element_type=F32)
                scores = jnp.where((rc >> l) == 1, pl_, scores)

            st = s_ref[hd]
            o = (jnp.dot(scores.astype(BF16), vh, preferred_element_type=F32)
                 + lax.dot_general((qh * jnp.exp(b)).astype(BF16), st.astype(BF16),
                                   (((1,), (1,)), ((), ())), preferred_element_type=F32))
            kz = (kh * jnp.exp(suffix)).astype(BF16)
            s_ref[hd] = (jnp.exp(b[CHUNK - 1:CHUNK, :]) * st
                         + lax.dot_general(vh, kz, (((0,), (0,)), ((), ())),
                                           preferred_element_type=F32))

            on = _rms(o, gn_ref[...])
            gate = p_ref[rows, off_g + hd * HEAD_V:off_g + (hd + 1) * HEAD_V]
            y_ref[rows, D_CONV + hd * HEAD_V:D_CONV + (hd + 1) * HEAD_V] = (
                on * (gate * _sigmoid(gate))).astype(BF16)
        return carry

    lax.fori_loop(0, tm // CHUNK, chunk_step, 0)

    o_ref[...] = x_ref[...] + jnp.dot(y_ref[...], wo_ref[...], preferred_element_type=F32)


def _mix(proj, z, x, conv_w, gw2, gate_b, gla_norm, dmat, w_out, *, tm=256):
    t = x.shape[0]
    const = lambda i: (0, 0)
    return pl.pallas_call(
        functools.partial(_mix_body, tm=tm),
        grid=(t // tm,),
        in_specs=[
            pl.BlockSpec((tm, MAIN_COLS), lambda i: (i, 0)),
            pl.BlockSpec((tm, LANE), lambda i: (i, 0)),
            pl.BlockSpec((tm, D_MODEL), lambda i: (i, 0)),
            pl.BlockSpec((CONV_WIDTH, D_CONV), const),
            pl.BlockSpec((LANE, D_QK), const),
            pl.BlockSpec((1, D_QK), const),
            pl.BlockSpec((1, HEAD_V), const),
            pl.BlockSpec((8 * CHUNK, CHUNK), const),
            pl.BlockSpec((D_MODEL, D_MODEL), const),
        ],
        out_specs=pl.BlockSpec((tm, D_MODEL), lambda i: (i, 0)),
        out_shape=jax.ShapeDtypeStruct((t, D_MODEL), F32),
        scratch_shapes=[
            pltpu.VMEM((N_HEADS, HEAD_V, HEAD_K), F32),
            pltpu.VMEM((tm + SUBLANE, D_CONV), F32),
            pltpu.VMEM((tm, D_QK), F32),
            pltpu.VMEM((tm, D_MODEL), BF16),
        ],
        compiler_params=pltpu.CompilerParams(
            dimension_semantics=("arbitrary",), vmem_limit_bytes=VMEM_LIMIT),
        name="mix",
    )(proj, z, x, conv_w, gw2, gate_b, gla_norm, dmat, w_out)


def kernel(x, ffn1_norm, ffn1_w_gate, ffn1_w_up, ffn1_w_down, mix_norm, w_in, conv_w, gate_w2,
           gate_b, gla_norm, w_out, ffn2_norm, ffn2_w_gate, ffn2_w_up, ffn2_w_down, final_norm):
    b, t, d = x.shape
    depth = w_in.shape[0]
    xs = x.reshape(b * t, d)
    dmat = jnp.asarray(_decay_sum_matrix(), BF16)
    fw = final_norm.reshape(1, d)
    for l in range(depth):
        xs = _ffn(xs, ffn1_norm[l].reshape(1, d), ffn1_w_gate[l].astype(BF16),
                  ffn1_w_up[l].astype(BF16), ffn1_w_down[l].astype(BF16), fw, final=False)
        w_main = w_in[l, :, :MAIN_COLS].astype(BF16)
        w_z = jnp.pad(w_in[l, :, MAIN_COLS:], ((0, 0), (0, LANE - GATE_RANK))).astype(BF16)
        proj, z = _inproj(xs, mix_norm[l].reshape(1, d), w_main, w_z)
        gw2 = jnp.pad(gate_w2[l], ((0, LANE - GATE_RANK), (0, 0))).astype(BF16)
        xs = _mix(proj, z, xs, conv_w[l], gw2, gate_b[l].reshape(1, D_QK),
                  gla_norm[l].reshape(1, HEAD_V), dmat, w_out[l].astype(BF16))
        xs = _ffn(xs, ffn2_norm[l].reshape(1, d), ffn2_w_gate[l].astype(BF16),
                  ffn2_w_up[l].astype(BF16), ffn2_w_down[l].astype(BF16), fw,
                  final=(l == depth - 1))
    return xs.reshape(b, t, d)
```

```python
import functools

import jax
import jax.numpy as jnp
import numpy as np
from jax import lax
from jax.experimental import pallas as pl
from jax.experimental.pallas import tpu as pltpu

D_MODEL = 2048
CHUNK = 64
D_CONV = 1024
D_GLA = 1024
N_HEADS = 4
HEAD_V = 256
HEAD_K = 128
D_QK = 512
GATE_RANK = 16
GATE_TAU = 16.0
CONV_WIDTH = 3
D_FF = 5632
EPS = 1e-6
MAIN_COLS = 3 * D_CONV + 2 * D_QK + 2 * D_GLA
LANE = 128
SUBLANE = 8
N_LEVELS = 6
VMEM_LIMIT = 60 * 1024 * 1024

OFF_Q = 3 * D_CONV
OFF_K = OFF_Q + D_QK
OFF_V = OFF_K + D_QK
OFF_G = OFF_V + D_GLA

F32 = jnp.float32
BF16 = jnp.bfloat16


def _rms(x, w):
    ms = jnp.mean(x * x, axis=-1, keepdims=True)
    return x * lax.rsqrt(ms + EPS) * w


def _sigmoid(x):
    return 1.0 / (1.0 + jnp.exp(-x))


def _dot_nt(a, b):
    return lax.dot_general(a, b, (((1,), (1,)), ((), ())), preferred_element_type=F32)


def _dot_tn(a, b):
    return lax.dot_general(a, b, (((0,), (0,)), ((), ())), preferred_element_type=F32)


def _ffn_body(x_ref, nw_ref, wg_ref, wu_ref, wd_ref, fw_ref, o_ref, h_ref, *, n_f, final):
    j = pl.program_id(1)

    @pl.when(j == 0)
    def _():
        h_ref[...] = _rms(x_ref[...], nw_ref[...]).astype(BF16)
        o_ref[...] = jnp.zeros_like(o_ref)

    h = h_ref[...]
    g = jnp.dot(h, wg_ref[...], preferred_element_type=F32)
    u = jnp.dot(h, wu_ref[...], preferred_element_type=F32)
    a = (g * _sigmoid(g) * u).astype(BF16)
    o_ref[...] += jnp.dot(a, wd_ref[...], preferred_element_type=F32)

    @pl.when(j == n_f - 1)
    def _():
        r = x_ref[...] + 0.5 * o_ref[...]
        if final:
            r = _rms(r, fw_ref[...])
        o_ref[...] = r


def _ffn(x, nw, wg, wu, wd, fw, *, final, tm=512, tf=512):
    t = x.shape[0]
    n_f = D_FF // tf
    return pl.pallas_call(
        functools.partial(_ffn_body, n_f=n_f, final=final),
        grid=(t // tm, n_f),
        in_specs=[
            pl.BlockSpec((tm, D_MODEL), lambda i, j: (i, 0)),
            pl.BlockSpec((1, D_MODEL), lambda i, j: (0, 0)),
            pl.BlockSpec((D_MODEL, tf), lambda i, j: (0, j)),
            pl.BlockSpec((D_MODEL, tf), lambda i, j: (0, j)),
            pl.BlockSpec((tf, D_MODEL), lambda i, j: (j, 0)),
            pl.BlockSpec((1, D_MODEL), lambda i, j: (0, 0)),
        ],
        out_specs=pl.BlockSpec((tm, D_MODEL), lambda i, j: (i, 0)),
        out_shape=jax.ShapeDtypeStruct((t, D_MODEL), F32),
        scratch_shapes=[pltpu.VMEM((tm, D_MODEL), BF16)],
        compiler_params=pltpu.CompilerParams(
            dimension_semantics=("arbitrary", "arbitrary"), vmem_limit_bytes=VMEM_LIMIT),
        name="ffn_final" if final else "ffn",
    )(x, nw, wg, wu, wd, fw)


def _chunk_tril(tm):
    t = np.arange(tm)[:, None]
    i = np.arange(tm)[None, :]
    return ((i <= t) & (i // CHUNK == t // CHUNK)).astype(np.float32)


def _group_row(v, group, pick):
    rows, cols = v.shape
    g = v.reshape(rows // group, group, cols)[:, pick:pick + 1, :]
    return jnp.broadcast_to(g, (rows // group, group, cols)).reshape(rows, cols)


def _mix_body(x_ref, nw_ref, win_ref, wz_ref, cw_ref, gw2_ref, gb_ref, gn_ref, tri_ref, wo_ref,
              o_ref, s_ref, ubuf_ref, p_ref, lap_ref, eb_ref, qs_ref, ks_ref, og_ref, y_ref, *, tm):
    i = pl.program_id(0)

    @pl.when(i == 0)
    def _():
        s_ref[...] = jnp.zeros_like(s_ref)
        ubuf_ref[0:SUBLANE, :] = jnp.zeros((SUBLANE, D_CONV), F32)
        lap_ref[0:SUBLANE, :] = jnp.zeros((SUBLANE, D_QK), F32)
        lap_ref[SUBLANE + tm:2 * SUBLANE + tm, :] = jnp.zeros((SUBLANE, D_QK), F32)

    h = _rms(x_ref[...], nw_ref[...]).astype(BF16)
    p_ref[...] = jnp.dot(h, win_ref[...], preferred_element_type=F32)
    zl = jnp.dot(h, wz_ref[...], preferred_element_type=F32)

    u = p_ref[:, 2 * D_CONV:3 * D_CONV] * p_ref[:, 0:D_CONV]
    ubuf_ref[SUBLANE:SUBLANE + tm, :] = u
    cw = cw_ref[...]
    conv = (cw[0:1, :] * ubuf_ref[SUBLANE - 2:SUBLANE - 2 + tm, :]
            + cw[1:2, :] * ubuf_ref[SUBLANE - 1:SUBLANE - 1 + tm, :]
            + cw[2:3, :] * u)
    y_ref[:, 0:D_CONV] = (p_ref[:, D_CONV:2 * D_CONV] * conv).astype(BF16)
    ubuf_ref[0:SUBLANE, :] = ubuf_ref[tm:tm + SUBLANE, :]

    zz = jnp.dot(zl.astype(BF16), gw2_ref[...], preferred_element_type=F32) + gb_ref[...]
    la = (jnp.minimum(zz, 0.0) - jnp.log(1.0 + jnp.exp(-jnp.abs(zz)))) * (1.0 / GATE_TAU)
    lap_ref[SUBLANE:SUBLANE + tm, :] = la

    la_hi = la.astype(BF16)
    r1 = la - la_hi.astype(F32)
    la_mid = r1.astype(BF16)
    la_lo = (r1 - la_mid.astype(F32)).astype(BF16)
    tri = tri_ref[...]
    b = (jnp.dot(tri, la_hi, preferred_element_type=F32)
         + jnp.dot(tri, la_mid, preferred_element_type=F32)
         + jnp.dot(tri, la_lo, preferred_element_type=F32))

    q = p_ref[:, OFF_Q:OFF_Q + D_QK] * (HEAD_K ** -0.5)
    k = p_ref[:, OFF_K:OFF_K + D_QK]
    rowi = lax.broadcasted_iota(jnp.int32, (tm, D_QK), 0)
    la1 = lap_ref[SUBLANE + 1:SUBLANE + 1 + tm, :]
    la2 = lap_ref[SUBLANE + 2:SUBLANE + 2 + tm, :]
    r4 = rowi & 3
    level_exponents = [
        jnp.where((rowi & 1) == 0, la1, 0.0),
        jnp.where(r4 == 0, la1 + la2, jnp.where(r4 == 1, la1, jnp.where(r4 == 3, la, 0.0))),
    ]
    for l in range(2, N_LEVELS):
        level_exponents.append(-jnp.abs(b - _group_row(b, 2 << l, 1 << l)))
    for l in range(N_LEVELS):
        gl = jnp.exp(level_exponents[l])
        qs_ref[l] = (q * gl).astype(BF16)
        ks_ref[l] = (k * gl).astype(BF16)
    qs_ref[N_LEVELS] = q.astype(BF16)
    ks_ref[N_LEVELS] = k.astype(BF16)
    b_last = _group_row(b, CHUNK, CHUNK - 1)
    eb_ref[...] = jnp.exp(b_last)
    qs_ref[N_LEVELS + 1] = (q * jnp.exp(b)).astype(BF16)
    ks_ref[N_LEVELS + 1] = (k * jnp.exp(b_last - b)).astype(BF16)

    row = lax.broadcasted_iota(jnp.int32, (CHUNK, CHUNK), 0)
    col = lax.broadcasted_iota(jnp.int32, (CHUNK, CHUNK), 1)
    rc = row ^ col

    def chunk_step(c, carry):
        r0 = pl.multiple_of(c * CHUNK, CHUNK)
        rows = pl.ds(r0, CHUNK)
        for hd in range(N_HEADS):
            ks = slice(hd * HEAD_K, (hd + 1) * HEAD_K)
            vs = slice(hd * HEAD_V, (hd + 1) * HEAD_V)
            scores = jnp.where(rc == 0,
                               _dot_nt(qs_ref[N_LEVELS, rows, ks], ks_ref[N_LEVELS, rows, ks]), 0.0)
            for l in range(N_LEVELS):
                scores = jnp.where((rc >> l) == 1,
                                   _dot_nt(qs_ref[l, rows, ks], ks_ref[l, rows, ks]), scores)
            vh = p_ref[rows, OFF_V + hd * HEAD_V:OFF_V + (hd + 1) * HEAD_V].astype(BF16)
            st = s_ref[hd]
            og_ref[rows, vs] = (jnp.dot(scores.astype(BF16), vh, preferred_element_type=F32)
                                + _dot_nt(qs_ref[N_LEVELS + 1, rows, ks], st.astype(BF16)))
            s_ref[hd] = (eb_ref[pl.ds(r0, SUBLANE), ks][0:1, :] * st
                         + _dot_tn(vh, ks_ref[N_LEVELS + 1, rows, ks]))
        return carry

    lax.fori_loop(0, tm // CHUNK, chunk_step, 0)

    for hd in range(N_HEADS):
        vs = slice(hd * HEAD_V, (hd + 1) * HEAD_V)
        gate = p_ref[:, OFF_G + hd * HEAD_V:OFF_G + (hd + 1) * HEAD_V]
        y_ref[:, D_CONV + hd * HEAD_V:D_CONV + (hd + 1) * HEAD_V] = (
            _rms(og_ref[:, vs], gn_ref[...]) * (gate * _sigmoid(gate))).astype(BF16)

    o_ref[...] = x_ref[...] + jnp.dot(y_ref[...], wo_ref[...], preferred_element_type=F32)


def _mix(x, nw, w_main, w_z, conv_w, gw2, gate_b, gla_norm, tri, w_out, *, tm):
    t = x.shape[0]

    def resident(shape):
        return pl.BlockSpec(shape, lambda i: (0,) * len(shape), pipeline_mode=pl.Buffered(1))

    return pl.pallas_call(
        functools.partial(_mix_body, tm=tm),
        grid=(t // tm,),
        in_specs=[
            pl.BlockSpec((tm, D_MODEL), lambda i: (i, 0)),
            resident((1, D_MODEL)),
            resident((D_MODEL, MAIN_COLS)),
            resident((D_MODEL, LANE)),
            resident((CONV_WIDTH, D_CONV)),
            resident((LANE, D_QK)),
            resident((1, D_QK)),
            resident((1, HEAD_V)),
            resident((tm, tm)),
            resident((D_MODEL, D_MODEL)),
        ],
        out_specs=pl.BlockSpec((tm, D_MODEL), lambda i: (i, 0)),
        out_shape=jax.ShapeDtypeStruct((t, D_MODEL), F32),
        scratch_shapes=[
            pltpu.VMEM((N_HEADS, HEAD_V, HEAD_K), F32),
            pltpu.VMEM((tm + SUBLANE, D_CONV), F32),
            pltpu.VMEM((tm, MAIN_COLS), F32),
            pltpu.VMEM((tm + 2 * SUBLANE, D_QK), F32),
            pltpu.VMEM((tm, D_QK), F32),
            pltpu.VMEM((N_LEVELS + 2, tm, D_QK), BF16),
            pltpu.VMEM((N_LEVELS + 2, tm, D_QK), BF16),
            pltpu.VMEM((tm, D_GLA), F32),
            pltpu.VMEM((tm, D_MODEL), BF16),
        ],
        compiler_params=pltpu.CompilerParams(
            dimension_semantics=("arbitrary",), vmem_limit_bytes=VMEM_LIMIT),
        name="mix",
    )(x, nw, w_main, w_z, conv_w, gw2, gate_b, gla_norm, tri, w_out)


def kernel(x, ffn1_norm, ffn1_w_gate, ffn1_w_up, ffn1_w_down, mix_norm, w_in, conv_w, gate_w2,
           gate_b, gla_norm, w_out, ffn2_norm, ffn2_w_gate, ffn2_w_up, ffn2_w_down, final_norm):
    b, t, d = x.shape
    depth = w_in.shape[0]
    mix_tm = 256
    xs = x.reshape(b * t, d)
    tri = jnp.asarray(_chunk_tril(mix_tm), BF16)
    fw = final_norm.reshape(1, d)
    for l in range(depth):
        xs = _ffn(xs, ffn1_norm[l].reshape(1, d), ffn1_w_gate[l].astype(BF16),
                  ffn1_w_up[l].astype(BF16), ffn1_w_down[l].astype(BF16), fw, final=False)
        w_main = w_in[l, :, :MAIN_COLS].astype(BF16)
        w_z = jnp.pad(w_in[l, :, MAIN_COLS:], ((0, 0), (0, LANE - GATE_RANK))).astype(BF16)
        gw2 = jnp.pad(gate_w2[l], ((0, LANE - GATE_RANK), (0, 0))).astype(BF16)
        xs = _mix(xs, mix_norm[l].reshape(1, d), w_main, w_z, conv_w[l], gw2,
                  gate_b[l].reshape(1, D_QK), gla_norm[l].reshape(1, HEAD_V), tri,
                  w_out[l].astype(BF16), tm=mix_tm)
        xs = _ffn(xs, ffn2_norm[l].reshape(1, d), ffn2_w_gate[l].astype(BF16),
                  ffn2_w_up[l].astype(BF16), ffn2_w_down[l].astype(BF16), fw,
                  final=(l == depth - 1))
    return xs.reshape(b, t, d)
```

```python
import functools

import jax
import jax.numpy as jnp
import numpy as np
from jax import lax
from jax.experimental import pallas as pl
from jax.experimental.pallas import tpu as pltpu

D_MODEL = 2048
CHUNK = 64
D_CONV = 1024
D_GLA = 1024
N_HEADS = 4
HEAD_V = 256
HEAD_K = 128
D_QK = 512
GATE_RANK = 16
GATE_TAU = 16.0
CONV_WIDTH = 3
D_FF = 5632
EPS = 1e-6
MAIN_COLS = 3 * D_CONV + 2 * D_QK + 2 * D_GLA
LANE = 128
SUBLANE = 8
N_LEVELS = 6
VMEM_LIMIT = 60 * 1024 * 1024
FFN_TM = 1024
FFN_TF = 512
FFN_SLAB = 256
MIX_TM = 256

OFF_Q = 3 * D_CONV
OFF_K = OFF_Q + D_QK
OFF_V = OFF_K + D_QK
OFF_G = OFF_V + D_GLA

F32 = jnp.float32
BF16 = jnp.bfloat16


def _rms(x, w):
    ms = jnp.mean(x * x, axis=-1, keepdims=True)
    return x * lax.rsqrt(ms + EPS) * w


def _sigmoid(x):
    return 1.0 / (1.0 + jnp.exp(-x))


def _dot_nt(a, b):
    return lax.dot_general(a, b, (((1,), (1,)), ((), ())), preferred_element_type=F32)


def _dot_tn(a, b):
    return lax.dot_general(a, b, (((0,), (0,)), ((), ())), preferred_element_type=F32)


def _ffn_body(x_ref, nw_ref, wg_ref, wu_ref, wd_ref, fw_ref, o_ref, h_ref, *, n_f, final):
    j = pl.program_id(1)

    n_slabs = x_ref.shape[0] // FFN_SLAB

    def slab(s):
        return pl.ds(pl.multiple_of(s * FFN_SLAB, FFN_SLAB), FFN_SLAB)

    @pl.when(j == 0)
    def _():
        def norm_slab(s, carry):
            h_ref[slab(s), :] = _rms(x_ref[slab(s), :], nw_ref[...]).astype(BF16)
            return carry
        lax.fori_loop(0, n_slabs, norm_slab, 0)
        o_ref[...] = jnp.zeros_like(o_ref)

    h = h_ref[...]
    g = jnp.dot(h, wg_ref[...], preferred_element_type=F32)
    u = jnp.dot(h, wu_ref[...], preferred_element_type=F32)
    a = (g * _sigmoid(g) * u).astype(BF16)
    o_ref[...] += jnp.dot(a, wd_ref[...], preferred_element_type=F32)

    @pl.when(j == n_f - 1)
    def _():
        def residual_slab(s, carry):
            r = x_ref[slab(s), :] + 0.5 * o_ref[slab(s), :]
            if final:
                r = _rms(r, fw_ref[...])
            o_ref[slab(s), :] = r
            return carry
        lax.fori_loop(0, n_slabs, residual_slab, 0)


def _ffn(x, nw, wg, wu, wd, fw, *, layer, final, tm=FFN_TM, tf=FFN_TF):
    t = x.shape[0]
    n_f = D_FF // tf
    return pl.pallas_call(
        functools.partial(_ffn_body, n_f=n_f, final=final),
        grid=(t // tm, n_f),
        in_specs=[
            pl.BlockSpec((tm, D_MODEL), lambda i, j: (i, 0)),
            pl.BlockSpec((1, D_MODEL), lambda i, j: (0, 0)),
            pl.BlockSpec((None, D_MODEL, tf), lambda i, j: (layer, 0, j)),
            pl.BlockSpec((None, D_MODEL, tf), lambda i, j: (layer, 0, j)),
            pl.BlockSpec((None, tf, D_MODEL), lambda i, j: (layer, j, 0)),
            pl.BlockSpec((1, D_MODEL), lambda i, j: (0, 0)),
        ],
        out_specs=pl.BlockSpec((tm, D_MODEL), lambda i, j: (i, 0)),
        out_shape=jax.ShapeDtypeStruct((t, D_MODEL), F32),
        scratch_shapes=[pltpu.VMEM((tm, D_MODEL), BF16)],
        compiler_params=pltpu.CompilerParams(
            dimension_semantics=("arbitrary", "arbitrary"), vmem_limit_bytes=VMEM_LIMIT),
        name="ffn_final" if final else "ffn",
    )(x, nw, wg, wu, wd, fw)


def _chunk_tril(tm):
    t = np.arange(tm)[:, None]
    i = np.arange(tm)[None, :]
    return ((i <= t) & (i // CHUNK == t // CHUNK)).astype(np.float32)


def _group_row(v, group, pick):
    rows, cols = v.shape
    g = v.reshape(rows // group, group, cols)[:, pick:pick + 1, :]
    return jnp.broadcast_to(g, (rows // group, group, cols)).reshape(rows, cols)


def _mix_body(x_ref, nw_ref, win_ref, wz_ref, cw_ref, gw2_ref, gb_ref, gn_ref, tri_ref, wo_ref,
              o_ref, s_ref, ubuf_ref, p_ref, lap_ref, eb_ref, qs_ref, ks_ref, og_ref, y_ref, *, tm):
    i = pl.program_id(0)

    @pl.when(i == 0)
    def _():
        s_ref[...] = jnp.zeros_like(s_ref)
        ubuf_ref[0:SUBLANE, :] = jnp.zeros((SUBLANE, D_CONV), F32)
        lap_ref[0:SUBLANE, :] = jnp.zeros((SUBLANE, D_QK), F32)
        lap_ref[SUBLANE + tm:2 * SUBLANE + tm, :] = jnp.zeros((SUBLANE, D_QK), F32)

    h = _rms(x_ref[...], nw_ref[...]).astype(BF16)
    p_ref[...] = jnp.dot(h, win_ref[...], preferred_element_type=F32)
    zl = jnp.dot(h, wz_ref[...], preferred_element_type=F32)

    u = p_ref[:, 2 * D_CONV:3 * D_CONV] * p_ref[:, 0:D_CONV]
    ubuf_ref[SUBLANE:SUBLANE + tm, :] = u
    cw = cw_ref[...]
    conv = (cw[0:1, :] * ubuf_ref[SUBLANE - 2:SUBLANE - 2 + tm, :]
            + cw[1:2, :] * ubuf_ref[SUBLANE - 1:SUBLANE - 1 + tm, :]
            + cw[2:3, :] * u)
    y_ref[:, 0:D_CONV] = (p_ref[:, D_CONV:2 * D_CONV] * conv).astype(BF16)
    ubuf_ref[0:SUBLANE, :] = ubuf_ref[tm:tm + SUBLANE, :]

    zz = jnp.dot(zl.astype(BF16), gw2_ref[...], preferred_element_type=F32) + gb_ref[...]
    la = (jnp.minimum(zz, 0.0) - jnp.log(1.0 + jnp.exp(-jnp.abs(zz)))) * (1.0 / GATE_TAU)
    lap_ref[SUBLANE:SUBLANE + tm, :] = la

    la_hi = la.astype(BF16)
    r1 = la - la_hi.astype(F32)
    la_mid = r1.astype(BF16)
    la_lo = (r1 - la_mid.astype(F32)).astype(BF16)
    tri = tri_ref[...]
    b = (jnp.dot(tri, la_hi, preferred_element_type=F32)
         + jnp.dot(tri, la_mid, preferred_element_type=F32)
         + jnp.dot(tri, la_lo, preferred_element_type=F32))

    q = p_ref[:, OFF_Q:OFF_Q + D_QK] * (HEAD_K ** -0.5)
    k = p_ref[:, OFF_K:OFF_K + D_QK]
    rowi = lax.broadcasted_iota(jnp.int32, (tm, D_QK), 0)
    la1 = lap_ref[SUBLANE + 1:SUBLANE + 1 + tm, :]
    la2 = lap_ref[SUBLANE + 2:SUBLANE + 2 + tm, :]
    r4 = rowi & 3
    level_exponents = [
        jnp.where((rowi & 1) == 0, la1, 0.0),
        jnp.where(r4 == 0, la1 + la2, jnp.where(r4 == 1, la1, jnp.where(r4 == 3, la, 0.0))),
    ]
    for l in range(2, N_LEVELS):
        level_exponents.append(-jnp.abs(b - _group_row(b, 2 << l, 1 << l)))
    for l in range(N_LEVELS):
        gl = jnp.exp(level_exponents[l])
        qs_ref[l] = (q * gl).astype(BF16)
        ks_ref[l] = (k * gl).astype(BF16)
    qs_ref[N_LEVELS] = q.astype(BF16)
    ks_ref[N_LEVELS] = k.astype(BF16)
    b_last = _group_row(b, CHUNK, CHUNK - 1)
    eb_ref[...] = jnp.exp(b_last)
    qs_ref[N_LEVELS + 1] = (q * jnp.exp(b)).astype(BF16)
    ks_ref[N_LEVELS + 1] = (k * jnp.exp(b_last - b)).astype(BF16)

    row = lax.broadcasted_iota(jnp.int32, (CHUNK, CHUNK), 0)
    col = lax.broadcasted_iota(jnp.int32, (CHUNK, CHUNK), 1)
    rc = row ^ col

    def chunk_step(c, carry):
        r0 = pl.multiple_of(c * CHUNK, CHUNK)
        rows = pl.ds(r0, CHUNK)
        for hd in range(N_HEADS):
            ks = slice(hd * HEAD_K, (hd + 1) * HEAD_K)
            vs = slice(hd * HEAD_V, (hd + 1) * HEAD_V)
            scores = jnp.where(rc == 0,
                               _dot_nt(qs_ref[N_LEVELS, rows, ks], ks_ref[N_LEVELS, rows, ks]), 0.0)
            for l in range(N_LEVELS):
                scores = jnp.where((rc >> l) == 1,
                                   _dot_nt(qs_ref[l, rows, ks], ks_ref[l, rows, ks]), scores)
            vh = p_ref[rows, OFF_V + hd * HEAD_V:OFF_V + (hd + 1) * HEAD_V].astype(BF16)
            st = s_ref[hd]
            og_ref[rows, vs] = (jnp.dot(scores.astype(BF16), vh, preferred_element_type=F32)
                                + _dot_nt(qs_ref[N_LEVELS + 1, rows, ks], st.astype(BF16)))
            s_ref[hd] = (eb_ref[pl.ds(r0, SUBLANE), ks][0:1, :] * st
                         + _dot_tn(vh, ks_ref[N_LEVELS + 1, rows, ks]))
        return carry

    lax.fori_loop(0, tm // CHUNK, chunk_step, 0)

    for hd in range(N_HEADS):
        vs = slice(hd * HEAD_V, (hd + 1) * HEAD_V)
        gate = p_ref[:, OFF_G + hd * HEAD_V:OFF_G + (hd + 1) * HEAD_V]
        y_ref[:, D_CONV + hd * HEAD_V:D_CONV + (hd + 1) * HEAD_V] = (
            _rms(og_ref[:, vs], gn_ref[...]) * (gate * _sigmoid(gate))).astype(BF16)

    o_ref[...] = x_ref[...] + jnp.dot(y_ref[...], wo_ref[...], preferred_element_type=F32)


def _mix(x, nw, w_in, w_z, conv_w, gw2, gate_b, gla_norm, tri, w_out, *, layer, tm):
    t = x.shape[0]

    def resident(shape):
        return pl.BlockSpec(shape, lambda i: (0,) * len(shape), pipeline_mode=pl.Buffered(1))

    def resident_layer(shape):
        return pl.BlockSpec((None,) + shape, lambda i: (layer,) + (0,) * len(shape),
                            pipeline_mode=pl.Buffered(1))

    return pl.pallas_call(
        functools.partial(_mix_body, tm=tm),
        grid=(t // tm,),
        in_specs=[
            pl.BlockSpec((tm, D_MODEL), lambda i: (i, 0)),
            resident((1, D_MODEL)),
            resident_layer((D_MODEL, MAIN_COLS)),
            resident((D_MODEL, LANE)),
            resident((CONV_WIDTH, D_CONV)),
            resident((LANE, D_QK)),
            resident((1, D_QK)),
            resident((1, HEAD_V)),
            resident((tm, tm)),
            resident_layer((D_MODEL, D_MODEL)),
        ],
        out_specs=pl.BlockSpec((tm, D_MODEL), lambda i: (i, 0)),
        out_shape=jax.ShapeDtypeStruct((t, D_MODEL), F32),
        scratch_shapes=[
            pltpu.VMEM((N_HEADS, HEAD_V, HEAD_K), F32),
            pltpu.VMEM((tm + SUBLANE, D_CONV), F32),
            pltpu.VMEM((tm, MAIN_COLS), F32),
            pltpu.VMEM((tm + 2 * SUBLANE, D_QK), F32),
            pltpu.VMEM((tm, D_QK), F32),
            pltpu.VMEM((N_LEVELS + 2, tm, D_QK), BF16),
            pltpu.VMEM((N_LEVELS + 2, tm, D_QK), BF16),
            pltpu.VMEM((tm, D_GLA), F32),
            pltpu.VMEM((tm, D_MODEL), BF16),
        ],
        compiler_params=pltpu.CompilerParams(
            dimension_semantics=("arbitrary",), vmem_limit_bytes=VMEM_LIMIT),
        name="mix",
    )(x, nw, w_in, w_z, conv_w, gw2, gate_b, gla_norm, tri, w_out)


def kernel(x, ffn1_norm, ffn1_w_gate, ffn1_w_up, ffn1_w_down, mix_norm, w_in, conv_w, gate_w2,
           gate_b, gla_norm, w_out, ffn2_norm, ffn2_w_gate, ffn2_w_up, ffn2_w_down, final_norm):
    b, t, d = x.shape
    depth = w_in.shape[0]
    xs = x.reshape(b * t, d)
    tri = jnp.asarray(_chunk_tril(MIX_TM), BF16)
    fw = final_norm.reshape(1, d)
    ffn1 = [w.astype(BF16) for w in (ffn1_w_gate, ffn1_w_up, ffn1_w_down)]
    ffn2 = [w.astype(BF16) for w in (ffn2_w_gate, ffn2_w_up, ffn2_w_down)]
    w_in_bf = w_in.astype(BF16)
    w_out_bf = w_out.astype(BF16)
    for l in range(depth):
        xs = _ffn(xs, ffn1_norm[l].reshape(1, d), *ffn1, fw, layer=l, final=False)
        w_z = jnp.pad(w_in[l, :, MAIN_COLS:], ((0, 0), (0, LANE - GATE_RANK))).astype(BF16)
        gw2 = jnp.pad(gate_w2[l], ((0, LANE - GATE_RANK), (0, 0))).astype(BF16)
        xs = _mix(xs, mix_norm[l].reshape(1, d), w_in_bf, w_z, conv_w[l], gw2,
                  gate_b[l].reshape(1, D_QK), gla_norm[l].reshape(1, HEAD_V), tri,
                  w_out_bf, layer=l, tm=MIX_TM)
        xs = _ffn(xs, ffn2_norm[l].reshape(1, d), *ffn2, fw, layer=l, final=(l == depth - 1))
    return xs.reshape(b, t, d)
```

```python
import functools

import jax
import jax.numpy as jnp
import numpy as np
from jax import lax
from jax.experimental import pallas as pl
from jax.experimental.pallas import tpu as pltpu

D_MODEL = 2048
CHUNK = 64
D_CONV = 1024
D_GLA = 1024
N_HEADS = 4
HEAD_V = 256
HEAD_K = 128
D_QK = 512
GATE_RANK = 16
GATE_TAU = 16.0
CONV_WIDTH = 3
D_FF = 5632
EPS = 1e-6
MAIN_COLS = 3 * D_CONV + 2 * D_QK + 2 * D_GLA
LANE = 128
SUBLANE = 8
N_LEVELS = 6
MXU_N = 256
VMEM_LIMIT = 60 * 1024 * 1024
FFN_TM = 1024
FFN_TF = 512
FFN_SLAB = 256
MIX_TM = 256

OFF_Q = 3 * D_CONV
OFF_K = OFF_Q + D_QK
OFF_V = OFF_K + D_QK
OFF_G = OFF_V + D_GLA

F32 = jnp.float32
BF16 = jnp.bfloat16


def _rms(x, w):
    ms = jnp.mean(x * x, axis=-1, keepdims=True)
    return x * lax.rsqrt(ms + EPS) * w


def _sigmoid(x):
    return 1.0 / (1.0 + jnp.exp(-x))


def _dot_nt(a, b):
    return lax.dot_general(a, b, (((1,), (1,)), ((), ())), preferred_element_type=F32)


def _dot_tn(a, b):
    return lax.dot_general(a, b, (((0,), (0,)), ((), ())), preferred_element_type=F32)


def _ffn_body(x_ref, nw_ref, wg_ref, wu_ref, wd_ref, fw_ref, o_ref, h_ref, *, n_f, final):
    j = pl.program_id(1)

    n_slabs = x_ref.shape[0] // FFN_SLAB

    def slab(s):
        return pl.ds(pl.multiple_of(s * FFN_SLAB, FFN_SLAB), FFN_SLAB)

    @pl.when(j == 0)
    def _():
        def norm_slab(s, carry):
            h_ref[slab(s), :] = _rms(x_ref[slab(s), :], nw_ref[...]).astype(BF16)
            return carry
        lax.fori_loop(0, n_slabs, norm_slab, 0)
        o_ref[...] = jnp.zeros_like(o_ref)

    h = h_ref[...]
    g = jnp.dot(h, wg_ref[...], preferred_element_type=F32)
    u = jnp.dot(h, wu_ref[...], preferred_element_type=F32)
    a = (g * _sigmoid(g) * u).astype(BF16)
    o_ref[...] += jnp.dot(a, wd_ref[...], preferred_element_type=F32)

    @pl.when(j == n_f - 1)
    def _():
        def residual_slab(s, carry):
            r = x_ref[slab(s), :] + 0.5 * o_ref[slab(s), :]
            if final:
                r = _rms(r, fw_ref[...])
            o_ref[slab(s), :] = r
            return carry
        lax.fori_loop(0, n_slabs, residual_slab, 0)


def _ffn(x, nw, wg, wu, wd, fw, *, layer, final, tm=FFN_TM, tf=FFN_TF):
    t = x.shape[0]
    n_f = D_FF // tf
    return pl.pallas_call(
        functools.partial(_ffn_body, n_f=n_f, final=final),
        grid=(t // tm, n_f),
        in_specs=[
            pl.BlockSpec((tm, D_MODEL), lambda i, j: (i, 0)),
            pl.BlockSpec((1, D_MODEL), lambda i, j: (0, 0)),
            pl.BlockSpec((None, D_MODEL, tf), lambda i, j: (layer, 0, j)),
            pl.BlockSpec((None, D_MODEL, tf), lambda i, j: (layer, 0, j)),
            pl.BlockSpec((None, tf, D_MODEL), lambda i, j: (layer, j, 0)),
            pl.BlockSpec((1, D_MODEL), lambda i, j: (0, 0)),
        ],
        out_specs=pl.BlockSpec((tm, D_MODEL), lambda i, j: (i, 0)),
        out_shape=jax.ShapeDtypeStruct((t, D_MODEL), F32),
        scratch_shapes=[pltpu.VMEM((tm, D_MODEL), BF16)],
        compiler_params=pltpu.CompilerParams(
            dimension_semantics=("arbitrary", "arbitrary"), vmem_limit_bytes=VMEM_LIMIT),
        name="ffn_final" if final else "ffn",
    )(x, nw, wg, wu, wd, fw)


def _chunk_tril(tm):
    t = np.arange(tm)[:, None]
    i = np.arange(tm)[None, :]
    return ((i <= t) & (i // CHUNK == t // CHUNK)).astype(np.float32)


def _group_row(v, group, pick):
    rows, cols = v.shape
    g = v.reshape(rows // group, group, cols)[:, pick:pick + 1, :]
    return jnp.broadcast_to(g, (rows // group, group, cols)).reshape(rows, cols)


def _mix_body(x_ref, nw_ref, win_ref, wz_ref, cw_ref, gw2_ref, gb_ref, gn_ref, tri_ref, wo_ref,
              o_ref, s_ref, ubuf_ref, pc_ref, pqk_ref, pvg_ref, lap_ref, eb_ref, qs_ref, ks_ref,
              og_ref, y_ref, xprev_ref, b_ref, *, tm):
    i = pl.program_id(0)

    @pl.when(i == 0)
    def _():
        for ref in (s_ref, pvg_ref, eb_ref, qs_ref, ks_ref, y_ref, xprev_ref):
            ref[...] = jnp.zeros_like(ref)
        ubuf_ref[0:SUBLANE, :] = jnp.zeros((SUBLANE, D_CONV), F32)
        lap_ref[0:SUBLANE, :] = jnp.zeros((SUBLANE, D_QK), F32)
        lap_ref[SUBLANE + tm:2 * SUBLANE + tm, :] = jnp.zeros((SUBLANE, D_QK), F32)

    h = _rms(x_ref[...], nw_ref[...]).astype(BF16)
    row = lax.broadcasted_iota(jnp.int32, (CHUNK, CHUNK), 0)
    col = lax.broadcasted_iota(jnp.int32, (CHUNK, CHUNK), 1)
    rc = row ^ col

    def project(dst_ref, col0, blk):
        cs = slice(blk * MXU_N, (blk + 1) * MXU_N)
        dst_ref[:, cs] = jnp.dot(h, win_ref[:, col0 + blk * MXU_N:col0 + (blk + 1) * MXU_N],
                                 preferred_element_type=F32)

    def head_chunk(c, hd):
        rows = slice(c * CHUNK, (c + 1) * CHUNK)
        ks = slice(hd * HEAD_K, (hd + 1) * HEAD_K)
        vs = slice(hd * HEAD_V, (hd + 1) * HEAD_V)
        scores = jnp.where(rc == 0,
                           _dot_nt(qs_ref[N_LEVELS, rows, ks], ks_ref[N_LEVELS, rows, ks]), 0.0)
        for l in range(N_LEVELS):
            scores = jnp.where((rc >> l) == 1,
                               _dot_nt(qs_ref[l, rows, ks], ks_ref[l, rows, ks]), scores)
        vh = pvg_ref[rows, vs].astype(BF16)
        st = s_ref[hd]
        og_ref[rows, vs] = (jnp.dot(scores.astype(BF16), vh, preferred_element_type=F32)
                            + _dot_nt(qs_ref[N_LEVELS + 1, rows, ks], st.astype(BF16)))
        s_ref[hd] = (eb_ref[c * SUBLANE:c * SUBLANE + 1, ks] * st
                     + _dot_tn(vh, ks_ref[N_LEVELS + 1, rows, ks]))

    def wout_block(blk):
        cs = slice(blk * MXU_N, (blk + 1) * MXU_N)
        o_ref[:, cs] = xprev_ref[:, cs] + jnp.dot(y_ref[...], wo_ref[:, cs],
                                                  preferred_element_type=F32)

    def conv_block(blk):
        cs = slice(blk * MXU_N, (blk + 1) * MXU_N)
        u = pc_ref[:, 2 * D_CONV + blk * MXU_N:2 * D_CONV + (blk + 1) * MXU_N] * pc_ref[:, cs]
        ubuf_ref[SUBLANE:SUBLANE + tm, cs] = u
        cw = cw_ref[:, cs]
        conv = (cw[0:1, :] * ubuf_ref[SUBLANE - 2:SUBLANE - 2 + tm, cs]
                + cw[1:2, :] * ubuf_ref[SUBLANE - 1:SUBLANE - 1 + tm, cs]
                + cw[2:3, :] * u)
        y_ref[:, cs] = (pc_ref[:, D_CONV + blk * MXU_N:D_CONV + (blk + 1) * MXU_N] * conv).astype(BF16)
        ubuf_ref[0:SUBLANE, cs] = ubuf_ref[tm:tm + SUBLANE, cs]

    def prepare_rows(c):
        r0 = c * CHUNK
        rows = slice(r0, r0 + CHUNK)
        la = lap_ref[SUBLANE + r0:SUBLANE + r0 + CHUNK, :]
        la1 = lap_ref[SUBLANE + r0 + 1:SUBLANE + r0 + 1 + CHUNK, :]
        la2 = lap_ref[SUBLANE + r0 + 2:SUBLANE + r0 + 2 + CHUNK, :]
        b = b_ref[rows, :]
        q = pqk_ref[rows, 0:D_QK] * (HEAD_K ** -0.5)
        k = pqk_ref[rows, D_QK:2 * D_QK]
        rowi = lax.broadcasted_iota(jnp.int32, (CHUNK, D_QK), 0)
        r4 = rowi & 3
        for l in range(N_LEVELS):
            if l == 0:
                exponent = jnp.where((rowi & 1) == 0, la1, 0.0)
            elif l == 1:
                exponent = jnp.where(r4 == 0, la1 + la2,
                                     jnp.where(r4 == 1, la1, jnp.where(r4 == 3, la, 0.0)))
            else:
                exponent = -jnp.abs(b - _group_row(b, 2 << l, 1 << l))
            gl = jnp.exp(exponent)
            qs_ref[l, rows, :] = (q * gl).astype(BF16)
            ks_ref[l, rows, :] = (k * gl).astype(BF16)
        qs_ref[N_LEVELS, rows, :] = q.astype(BF16)
        ks_ref[N_LEVELS, rows, :] = k.astype(BF16)
        b_last = jnp.broadcast_to(b[CHUNK - 1:CHUNK, :], (CHUNK, D_QK))
        eb_ref[c * SUBLANE:(c + 1) * SUBLANE, :] = jnp.exp(b_last[0:SUBLANE, :])
        qs_ref[N_LEVELS + 1, rows, :] = (q * jnp.exp(b)).astype(BF16)
        ks_ref[N_LEVELS + 1, rows, :] = (k * jnp.exp(b_last - b)).astype(BF16)

    zl = jnp.dot(h, wz_ref[...], preferred_element_type=F32)
    zz = jnp.dot(zl.astype(BF16), gw2_ref[...], preferred_element_type=F32) + gb_ref[...]
    la = (jnp.minimum(zz, 0.0) - jnp.log(1.0 + jnp.exp(-jnp.abs(zz)))) * (1.0 / GATE_TAU)
    lap_ref[SUBLANE:SUBLANE + tm, :] = la
    la_hi = la.astype(BF16)
    r1 = la - la_hi.astype(F32)
    la_mid = r1.astype(BF16)
    la_lo = (r1 - la_mid.astype(F32)).astype(BF16)
    tri = tri_ref[...]
    b_ref[...] = (jnp.dot(tri, la_hi, preferred_element_type=F32)
                  + jnp.dot(tri, la_mid, preferred_element_type=F32)
                  + jnp.dot(tri, la_lo, preferred_element_type=F32))

    cq_blocks = ([(pqk_ref, OFF_Q, blk) for blk in range(2 * D_QK // MXU_N)]
                 + [(pc_ref, 0, blk) for blk in range(OFF_Q // MXU_N)])
    for c in range(tm // CHUNK):
        for hd in range(N_HEADS):
            project(*cq_blocks[c * N_HEADS + hd])
            head_chunk(c, hd)
        prepare_rows(c)

    for hd in range(N_HEADS):
        vs = slice(hd * HEAD_V, (hd + 1) * HEAD_V)
        gate = pvg_ref[:, D_GLA + hd * HEAD_V:D_GLA + (hd + 1) * HEAD_V]
        y_ref[:, D_CONV + hd * HEAD_V:D_CONV + (hd + 1) * HEAD_V] = (
            _rms(og_ref[:, vs], gn_ref[...]) * (gate * _sigmoid(gate))).astype(BF16)

    for blk in range(D_MODEL // MXU_N):
        wout_block(blk)
    xprev_ref[...] = x_ref[...]
    for blk in range(2 * D_GLA // MXU_N):
        project(pvg_ref, OFF_V, blk)
        if blk < D_CONV // MXU_N:
            conv_block(blk)


def _mix(x, nw, w_in, w_z, conv_w, gw2, gate_b, gla_norm, tri, w_out, *, layer, tm):
    t = x.shape[0]
    n = t // tm

    def resident(shape):
        return pl.BlockSpec(shape, lambda i: (0,) * len(shape), pipeline_mode=pl.Buffered(1))

    def resident_layer(shape):
        return pl.BlockSpec((None,) + shape, lambda i: (layer,) + (0,) * len(shape),
                            pipeline_mode=pl.Buffered(1))

    return pl.pallas_call(
        functools.partial(_mix_body, tm=tm),
        grid=(n + 1,),
        in_specs=[
            pl.BlockSpec((tm, D_MODEL), lambda i: (jnp.minimum(i, n - 1), 0)),
            resident((1, D_MODEL)),
            resident_layer((D_MODEL, MAIN_COLS)),
            resident((D_MODEL, LANE)),
            resident((CONV_WIDTH, D_CONV)),
            resident((LANE, D_QK)),
            resident((1, D_QK)),
            resident((1, HEAD_V)),
            resident((tm, tm)),
            resident_layer((D_MODEL, D_MODEL)),
        ],
        out_specs=pl.BlockSpec((tm, D_MODEL), lambda i: (jnp.maximum(i - 1, 0), 0)),
        out_shape=jax.ShapeDtypeStruct((t, D_MODEL), F32),
        scratch_shapes=[
            pltpu.VMEM((N_HEADS, HEAD_V, HEAD_K), F32),
            pltpu.VMEM((tm + SUBLANE, D_CONV), F32),
            pltpu.VMEM((tm, 3 * D_CONV), F32),
            pltpu.VMEM((tm, 2 * D_QK), F32),
            pltpu.VMEM((tm, 2 * D_GLA), F32),
            pltpu.VMEM((tm + 2 * SUBLANE, D_QK), F32),
            pltpu.VMEM((tm // CHUNK * SUBLANE, D_QK), F32),
            pltpu.VMEM((N_LEVELS + 2, tm, D_QK), BF16),
            pltpu.VMEM((N_LEVELS + 2, tm, D_QK), BF16),
            pltpu.VMEM((tm, D_GLA), F32),
            pltpu.VMEM((tm, D_MODEL), BF16),
            pltpu.VMEM((tm, D_MODEL), F32),
            pltpu.VMEM((tm, D_QK), F32),
        ],
        compiler_params=pltpu.CompilerParams(
            dimension_semantics=("arbitrary",), vmem_limit_bytes=VMEM_LIMIT),
        name="mix",
    )(x, nw, w_in, w_z, conv_w, gw2, gate_b, gla_norm, tri, w_out)


def kernel(x, ffn1_norm, ffn1_w_gate, ffn1_w_up, ffn1_w_down, mix_norm, w_in, conv_w, gate_w2,
           gate_b, gla_norm, w_out, ffn2_norm, ffn2_w_gate, ffn2_w_up, ffn2_w_down, final_norm):
    b, t, d = x.shape
    depth = w_in.shape[0]
    xs = x.reshape(b * t, d)
    tri = jnp.asarray(_chunk_tril(MIX_TM), BF16)
    fw = final_norm.reshape(1, d)
    ffn1 = [w.astype(BF16) for w in (ffn1_w_gate, ffn1_w_up, ffn1_w_down)]
    ffn2 = [w.astype(BF16) for w in (ffn2_w_gate, ffn2_w_up, ffn2_w_down)]
    w_in_bf = w_in.astype(BF16)
    w_out_bf = w_out.astype(BF16)
    for l in range(depth):
        xs = _ffn(xs, ffn1_norm[l].reshape(1, d), *ffn1, fw, layer=l, final=False)
        w_z = jnp.pad(w_in[l, :, MAIN_COLS:], ((0, 0), (0, LANE - GATE_RANK))).astype(BF16)
        gw2 = jnp.pad(gate_w2[l], ((0, LANE - GATE_RANK), (0, 0))).astype(BF16)
        xs = _mix(xs, mix_norm[l].reshape(1, d), w_in_bf, w_z, conv_w[l], gw2,
                  gate_b[l].reshape(1, D_QK), gla_norm[l].reshape(1, HEAD_V), tri,
                  w_out_bf, layer=l, tm=MIX_TM)
        xs = _ffn(xs, ffn2_norm[l].reshape(1, d), *ffn2, fw, layer=l, final=(l == depth - 1))
    return xs.reshape(b, t, d)
```

```python
import functools

import jax
import jax.numpy as jnp
import numpy as np
from jax import lax
from jax.experimental import pallas as pl
from jax.experimental.pallas import tpu as pltpu

D_MODEL = 2048
CHUNK = 64
D_CONV = 1024
D_GLA = 1024
N_HEADS = 4
HEAD_V = 256
HEAD_K = 128
D_QK = 512
GATE_RANK = 16
GATE_TAU = 16.0
CONV_WIDTH = 3
D_FF = 5632
EPS = 1e-6
MAIN_COLS = 3 * D_CONV + 2 * D_QK + 2 * D_GLA
LANE = 128
SUBLANE = 8
N_LEVELS = 6
MXU_N = 256
VMEM_LIMIT = 60 * 1024 * 1024
FFN_TM = 1024
FFN_TF = 512
FFN_SLAB = 256
MIX_TM = 256

OFF_Q = 3 * D_CONV
OFF_K = OFF_Q + D_QK
OFF_V = OFF_K + D_QK
OFF_G = OFF_V + D_GLA

F32 = jnp.float32
BF16 = jnp.bfloat16


def _rms(x, w):
    ms = jnp.mean(x * x, axis=-1, keepdims=True)
    return x * lax.rsqrt(ms + EPS) * w


def _sigmoid(x):
    return 1.0 / (1.0 + jnp.exp(-x))


def _dot_nt(a, b):
    return lax.dot_general(a, b, (((1,), (1,)), ((), ())), preferred_element_type=F32)


def _dot_tn(a, b):
    return lax.dot_general(a, b, (((0,), (0,)), ((), ())), preferred_element_type=F32)


def _ffn_body(x_ref, nw_ref, wg_ref, wu_ref, wd_ref, fw_ref, *refs, n_f, final, n_cast):
    cast_in, o_ref, cast_out, h_ref = (refs[:n_cast], refs[n_cast], refs[n_cast + 1:2 * n_cast + 1],
                                       refs[2 * n_cast + 1])
    j = pl.program_id(1)
    n_slabs = x_ref.shape[0] // FFN_SLAB

    for src_ref, dst_ref in zip(cast_in, cast_out, strict=True):
        dst_ref[...] = src_ref[...].astype(BF16)

    def swiglu_down(h):
        g = jnp.dot(h, wg_ref[...], preferred_element_type=F32)
        u = jnp.dot(h, wu_ref[...], preferred_element_type=F32)
        a = (g * _sigmoid(g) * u).astype(BF16)
        return jnp.dot(a, wd_ref[...], preferred_element_type=F32)

    @pl.when(j == 0)
    def _():
        for s in range(n_slabs):
            rows = slice(s * FFN_SLAB, (s + 1) * FFN_SLAB)
            h = _rms(x_ref[rows, :], nw_ref[...]).astype(BF16)
            h_ref[rows, :] = h
            o_ref[rows, :] = swiglu_down(h)

    @pl.when(jnp.logical_and(j > 0, j < n_f - 1))
    def _():
        o_ref[...] += swiglu_down(h_ref[...])

    @pl.when(j == n_f - 1)
    def _():
        for s in range(n_slabs):
            rows = slice(s * FFN_SLAB, (s + 1) * FFN_SLAB)
            r = x_ref[rows, :] + 0.5 * (o_ref[rows, :] + swiglu_down(h_ref[rows, :]))
            if final:
                r = _rms(r, fw_ref[...])
            o_ref[rows, :] = r


def _ffn(x, nw, wg, wu, wd, fw, *, final, cast=None, tm=FFN_TM, tf=FFN_TF):
    t = x.shape[0]
    n_i, n_f = t // tm, D_FF // tf
    rb = D_MODEL // n_i
    in_specs = [
        pl.BlockSpec((tm, D_MODEL), lambda i, j: (i, 0)),
        pl.BlockSpec((1, D_MODEL), lambda i, j: (0, 0)),
        pl.BlockSpec((D_MODEL, tf), lambda i, j: (0, j)),
        pl.BlockSpec((D_MODEL, tf), lambda i, j: (0, j)),
        pl.BlockSpec((tf, D_MODEL), lambda i, j: (j, 0)),
        pl.BlockSpec((1, D_MODEL), lambda i, j: (0, 0)),
    ]
    out_specs = [pl.BlockSpec((tm, D_MODEL), lambda i, j: (i, 0))]
    out_shape = [jax.ShapeDtypeStruct((t, D_MODEL), F32)]
    cast_args = ()
    if cast is not None:
        cast_args, layer = cast
        assert rb * n_i == D_MODEL and rb % (2 * SUBLANE) == 0 and rb % LANE == 0
        in_specs += [
            pl.BlockSpec((None, rb, tf), lambda i, j: (layer, i, j)),
            pl.BlockSpec((None, rb, tf), lambda i, j: (layer, i, j)),
            pl.BlockSpec((None, tf, rb), lambda i, j: (layer, j, i)),
        ]
        out_specs += [
            pl.BlockSpec((rb, tf), lambda i, j: (i, j)),
            pl.BlockSpec((rb, tf), lambda i, j: (i, j)),
            pl.BlockSpec((tf, rb), lambda i, j: (j, i)),
        ]
        out_shape += [
            jax.ShapeDtypeStruct((D_MODEL, D_FF), BF16),
            jax.ShapeDtypeStruct((D_MODEL, D_FF), BF16),
            jax.ShapeDtypeStruct((D_FF, D_MODEL), BF16),
        ]
    out = pl.pallas_call(
        functools.partial(_ffn_body, n_f=n_f, final=final, n_cast=len(cast_args)),
        grid=(n_i, n_f),
        in_specs=in_specs,
        out_specs=out_specs,
        out_shape=out_shape,
        scratch_shapes=[pltpu.VMEM((tm, D_MODEL), BF16)],
        compiler_params=pltpu.CompilerParams(
            dimension_semantics=("arbitrary", "arbitrary"), vmem_limit_bytes=VMEM_LIMIT),
        name="ffn_final" if final else "ffn",
    )(x, nw, wg, wu, wd, fw, *cast_args)
    return out[0], out[1:]


def _chunk_tril(tm):
    t = np.arange(tm)[:, None]
    i = np.arange(tm)[None, :]
    return ((i <= t) & (i // CHUNK == t // CHUNK)).astype(np.float32)


def _group_row(v, group, pick):
    rows, cols = v.shape
    g = v.reshape(rows // group, group, cols)[:, pick:pick + 1, :]
    return jnp.broadcast_to(g, (rows // group, group, cols)).reshape(rows, cols)


def _mix_body(x_ref, nw_ref, win_ref, wz_ref, cw_ref, gw2_ref, gb_ref, gn_ref, tri_ref, wo_ref,
              o_ref, s_ref, ubuf_ref, pc_ref, pqk_ref, pvg_ref, lap_ref, eb_ref, qs_ref, ks_ref,
              og_ref, y_ref, xprev_ref, b_ref, *, tm):
    i = pl.program_id(0)

    @pl.when(i == 0)
    def _():
        for ref in (s_ref, pvg_ref, eb_ref, qs_ref, ks_ref, y_ref, xprev_ref):
            ref[...] = jnp.zeros_like(ref)
        ubuf_ref[0:SUBLANE, :] = jnp.zeros((SUBLANE, D_CONV), F32)
        lap_ref[0:SUBLANE, :] = jnp.zeros((SUBLANE, D_QK), F32)
        lap_ref[SUBLANE + tm:2 * SUBLANE + tm, :] = jnp.zeros((SUBLANE, D_QK), F32)

    h = _rms(x_ref[...], nw_ref[...]).astype(BF16)
    row = lax.broadcasted_iota(jnp.int32, (CHUNK, CHUNK), 0)
    col = lax.broadcasted_iota(jnp.int32, (CHUNK, CHUNK), 1)
    rc = row ^ col

    def project(dst_ref, col0, blk):
        cs = slice(blk * MXU_N, (blk + 1) * MXU_N)
        dst_ref[:, cs] = jnp.dot(h, win_ref[:, col0 + blk * MXU_N:col0 + (blk + 1) * MXU_N],
                                 preferred_element_type=F32)

    def head_chunk(c, hd):
        rows = slice(c * CHUNK, (c + 1) * CHUNK)
        ks = slice(hd * HEAD_K, (hd + 1) * HEAD_K)
        vs = slice(hd * HEAD_V, (hd + 1) * HEAD_V)
        scores = jnp.where(rc == 0,
                           _dot_nt(qs_ref[N_LEVELS, rows, ks], ks_ref[N_LEVELS, rows, ks]), 0.0)
        for l in range(N_LEVELS):
            scores = jnp.where((rc >> l) == 1,
                               _dot_nt(qs_ref[l, rows, ks], ks_ref[l, rows, ks]), scores)
        vh = pvg_ref[rows, vs].astype(BF16)
        st = s_ref[hd]
        og_ref[rows, vs] = (jnp.dot(scores.astype(BF16), vh, preferred_element_type=F32)
                            + _dot_nt(qs_ref[N_LEVELS + 1, rows, ks], st.astype(BF16)))
        s_ref[hd] = (eb_ref[c * SUBLANE:c * SUBLANE + 1, ks] * st
                     + _dot_tn(vh, ks_ref[N_LEVELS + 1, rows, ks]))

    def wout_block(blk):
        cs = slice(blk * MXU_N, (blk + 1) * MXU_N)
        o_ref[:, cs] = xprev_ref[:, cs] + jnp.dot(y_ref[...], wo_ref[:, cs],
                                                  preferred_element_type=F32)

    def conv_block(blk):
        cs = slice(blk * MXU_N, (blk + 1) * MXU_N)
        u = pc_ref[:, 2 * D_CONV + blk * MXU_N:2 * D_CONV + (blk + 1) * MXU_N] * pc_ref[:, cs]
        ubuf_ref[SUBLANE:SUBLANE + tm, cs] = u
        cw = cw_ref[:, cs]
        conv = (cw[0:1, :] * ubuf_ref[SUBLANE - 2:SUBLANE - 2 + tm, cs]
                + cw[1:2, :] * ubuf_ref[SUBLANE - 1:SUBLANE - 1 + tm, cs]
                + cw[2:3, :] * u)
        y_ref[:, cs] = (pc_ref[:, D_CONV + blk * MXU_N:D_CONV + (blk + 1) * MXU_N] * conv).astype(BF16)
        ubuf_ref[0:SUBLANE, cs] = ubuf_ref[tm:tm + SUBLANE, cs]

    def prepare_rows(c):
        r0 = c * CHUNK
        rows = slice(r0, r0 + CHUNK)
        la = lap_ref[SUBLANE + r0:SUBLANE + r0 + CHUNK, :]
        la1 = lap_ref[SUBLANE + r0 + 1:SUBLANE + r0 + 1 + CHUNK, :]
        la2 = lap_ref[SUBLANE + r0 + 2:SUBLANE + r0 + 2 + CHUNK, :]
        b = b_ref[rows, :]
        q = pqk_ref[rows, 0:D_QK] * (HEAD_K ** -0.5)
        k = pqk_ref[rows, D_QK:2 * D_QK]
        rowi = lax.broadcasted_iota(jnp.int32, (CHUNK, D_QK), 0)
        r4 = rowi & 3
        for l in range(N_LEVELS):
            if l == 0:
                exponent = jnp.where((rowi & 1) == 0, la1, 0.0)
            elif l == 1:
                exponent = jnp.where(r4 == 0, la1 + la2,
                                     jnp.where(r4 == 1, la1, jnp.where(r4 == 3, la, 0.0)))
            else:
                exponent = -jnp.abs(b - _group_row(b, 2 << l, 1 << l))
            gl = jnp.exp(exponent)
            qs_ref[l, rows, :] = (q * gl).astype(BF16)
            ks_ref[l, rows, :] = (k * gl).astype(BF16)
        qs_ref[N_LEVELS, rows, :] = q.astype(BF16)
        ks_ref[N_LEVELS, rows, :] = k.astype(BF16)
        b_last = jnp.broadcast_to(b[CHUNK - 1:CHUNK, :], (CHUNK, D_QK))
        eb_ref[c * SUBLANE:(c + 1) * SUBLANE, :] = jnp.exp(b_last[0:SUBLANE, :])
        qs_ref[N_LEVELS + 1, rows, :] = (q * jnp.exp(b)).astype(BF16)
        ks_ref[N_LEVELS + 1, rows, :] = (k * jnp.exp(b_last - b)).astype(BF16)

    zl = jnp.dot(h, wz_ref[...], preferred_element_type=F32)
    zz = jnp.dot(zl.astype(BF16), gw2_ref[...], preferred_element_type=F32) + gb_ref[...]
    la = (jnp.minimum(zz, 0.0) - jnp.log(1.0 + jnp.exp(-jnp.abs(zz)))) * (1.0 / GATE_TAU)
    lap_ref[SUBLANE:SUBLANE + tm, :] = la
    la_hi = la.astype(BF16)
    r1 = la - la_hi.astype(F32)
    la_mid = r1.astype(BF16)
    la_lo = (r1 - la_mid.astype(F32)).astype(BF16)
    tri = tri_ref[...]
    b_ref[...] = (jnp.dot(tri, la_hi, preferred_element_type=F32)
                  + jnp.dot(tri, la_mid, preferred_element_type=F32)
                  + jnp.dot(tri, la_lo, preferred_element_type=F32))

    cq_blocks = ([(pqk_ref, OFF_Q, blk) for blk in range(2 * D_QK // MXU_N)]
                 + [(pc_ref, 0, blk) for blk in range(OFF_Q // MXU_N)])
    for c in range(tm // CHUNK):
        for hd in range(N_HEADS):
            project(*cq_blocks[c * N_HEADS + hd])
            head_chunk(c, hd)
        prepare_rows(c)

    for hd in range(N_HEADS):
        vs = slice(hd * HEAD_V, (hd + 1) * HEAD_V)
        gate = pvg_ref[:, D_GLA + hd * HEAD_V:D_GLA + (hd + 1) * HEAD_V]
        y_ref[:, D_CONV + hd * HEAD_V:D_CONV + (hd + 1) * HEAD_V] = (
            _rms(og_ref[:, vs], gn_ref[...]) * (gate * _sigmoid(gate))).astype(BF16)

    for blk in range(D_MODEL // MXU_N):
        wout_block(blk)
    xprev_ref[...] = x_ref[...]
    for blk in range(2 * D_GLA // MXU_N):
        project(pvg_ref, OFF_V, blk)
        if blk < D_CONV // MXU_N:
            conv_block(blk)


def _mix(x, nw, w_in, w_z, conv_w, gw2, gate_b, gla_norm, tri, w_out, *, layer, tm):
    t = x.shape[0]
    n = t // tm

    def resident(shape):
        return pl.BlockSpec(shape, lambda i: (0,) * len(shape), pipeline_mode=pl.Buffered(1))

    def resident_layer(shape):
        return pl.BlockSpec((None,) + shape, lambda i: (layer,) + (0,) * len(shape),
                            pipeline_mode=pl.Buffered(1))

    return pl.pallas_call(
        functools.partial(_mix_body, tm=tm),
        grid=(n + 1,),
        in_specs=[
            pl.BlockSpec((tm, D_MODEL), lambda i: (jnp.minimum(i, n - 1), 0)),
            resident((1, D_MODEL)),
            resident_layer((D_MODEL, MAIN_COLS)),
            resident((D_MODEL, LANE)),
            resident((CONV_WIDTH, D_CONV)),
            resident((LANE, D_QK)),
            resident((1, D_QK)),
            resident((1, HEAD_V)),
            resident((tm, tm)),
            resident_layer((D_MODEL, D_MODEL)),
        ],
        out_specs=pl.BlockSpec((tm, D_MODEL), lambda i: (jnp.maximum(i - 1, 0), 0)),
        out_shape=jax.ShapeDtypeStruct((t, D_MODEL), F32),
        scratch_shapes=[
            pltpu.VMEM((N_HEADS, HEAD_V, HEAD_K), F32),
            pltpu.VMEM((tm + SUBLANE, D_CONV), F32),
            pltpu.VMEM((tm, 3 * D_CONV), F32),
            pltpu.VMEM((tm, 2 * D_QK), F32),
            pltpu.VMEM((tm, 2 * D_GLA), F32),
            pltpu.VMEM((tm + 2 * SUBLANE, D_QK), F32),
            pltpu.VMEM((tm // CHUNK * SUBLANE, D_QK), F32),
            pltpu.VMEM((N_LEVELS + 2, tm, D_QK), BF16),
            pltpu.VMEM((N_LEVELS + 2, tm, D_QK), BF16),
            pltpu.VMEM((tm, D_GLA), F32),
            pltpu.VMEM((tm, D_MODEL), BF16),
            pltpu.VMEM((tm, D_MODEL), F32),
            pltpu.VMEM((tm, D_QK), F32),
        ],
        compiler_params=pltpu.CompilerParams(
            dimension_semantics=("arbitrary",), vmem_limit_bytes=VMEM_LIMIT),
        name="mix",
    )(x, nw, w_in, w_z, conv_w, gw2, gate_b, gla_norm, tri, w_out)


def kernel(x, ffn1_norm, ffn1_w_gate, ffn1_w_up, ffn1_w_down, mix_norm, w_in, conv_w, gate_w2,
           gate_b, gla_norm, w_out, ffn2_norm, ffn2_w_gate, ffn2_w_up, ffn2_w_down, final_norm):
    b, t, d = x.shape
    depth = w_in.shape[0]
    xs = x.reshape(b * t, d)
    tri = jnp.asarray(_chunk_tril(MIX_TM), BF16)
    fw = final_norm.reshape(1, d)
    ffn1 = (ffn1_w_gate, ffn1_w_up, ffn1_w_down)
    ffn2 = (ffn2_w_gate, ffn2_w_up, ffn2_w_down)
    w_in_bf = w_in.astype(BF16)
    w_out_bf = w_out.astype(BF16)
    weights = [w[0].astype(BF16) for w in ffn1]
    for l in range(depth):
        last = l == depth - 1
        xs, weights = _ffn(xs, ffn1_norm[l].reshape(1, d), *weights, fw, final=False, cast=(ffn2, l))
        w_z = jnp.pad(w_in[l, :, MAIN_COLS:], ((0, 0), (0, LANE - GATE_RANK))).astype(BF16)
        gw2 = jnp.pad(gate_w2[l], ((0, LANE - GATE_RANK), (0, 0))).astype(BF16)
        xs = _mix(xs, mix_norm[l].reshape(1, d), w_in_bf, w_z, conv_w[l], gw2,
                  gate_b[l].reshape(1, D_QK), gla_norm[l].reshape(1, HEAD_V), tri,
                  w_out_bf, layer=l, tm=MIX_TM)
        xs, weights = _ffn(xs, ffn2_norm[l].reshape(1, d), *weights, fw, final=last,
                           cast=None if last else (ffn1, l + 1))
    return xs.reshape(b, t, d)
```

```python
import functools

import jax
import jax.numpy as jnp
import numpy as np
from jax import lax
from jax.experimental import pallas as pl
from jax.experimental.pallas import tpu as pltpu

D_MODEL = 2048
CHUNK = 64
D_CONV = 1024
D_GLA = 1024
N_HEADS = 4
HEAD_V = 256
HEAD_K = 128
D_QK = 512
GATE_RANK = 16
GATE_TAU = 16.0
CONV_WIDTH = 3
D_FF = 5632
EPS = 1e-6
MAIN_COLS = 3 * D_CONV + 2 * D_QK + 2 * D_GLA
LANE = 128
SUBLANE = 8
N_LEVELS = 6
MXU_N = 256
VMEM_LIMIT = 60 * 1024 * 1024
FFN_TM = 1024
FFN_TF = 512
FFN_SLAB = 256
FFN_CAST_BLOCKS = 4
MIX_TM = 256

OFF_Q = 3 * D_CONV
OFF_K = OFF_Q + D_QK
OFF_V = OFF_K + D_QK
OFF_G = OFF_V + D_GLA

F32 = jnp.float32
BF16 = jnp.bfloat16


def _rms(x, w):
    ms = jnp.mean(x * x, axis=-1, keepdims=True)
    return x * lax.rsqrt(ms + EPS) * w


def _sigmoid(x):
    return 1.0 / (1.0 + jnp.exp(-x))


def _dot_nt(a, b):
    return lax.dot_general(a, b, (((1,), (1,)), ((), ())), preferred_element_type=F32)


def _dot_tn(a, b):
    return lax.dot_general(a, b, (((0,), (0,)), ((), ())), preferred_element_type=F32)


def _ffn_body(x_ref, nw_ref, wg_ref, wu_ref, wd_ref, fw_ref, *refs, n_f, final, n_cast, n_slow_cast):
    n_in = n_cast + n_slow_cast
    cast_in, slow_in = refs[:n_cast], refs[n_cast:n_in]
    o_ref = refs[n_in]
    cast_out, slow_out = refs[n_in + 1:n_in + 1 + n_cast], refs[n_in + 1 + n_cast:2 * n_in + 1]
    h_ref = refs[2 * n_in + 1]
    j = pl.program_id(1)
    n_slabs = x_ref.shape[0] // FFN_SLAB

    for src_ref, dst_ref in zip(cast_in, cast_out, strict=True):
        dst_ref[...] = src_ref[...].astype(BF16)

    if n_slow_cast:
        @pl.when((j * FFN_CAST_BLOCKS) % n_f < FFN_CAST_BLOCKS)
        def _():
            for src_ref, dst_ref in zip(slow_in, slow_out, strict=True):
                dst_ref[...] = src_ref[...].astype(BF16)

    def swiglu_down(h):
        g = jnp.dot(h, wg_ref[...], preferred_element_type=F32)
        u = jnp.dot(h, wu_ref[...], preferred_element_type=F32)
        a = (g * _sigmoid(g) * u).astype(BF16)
        return jnp.dot(a, wd_ref[...], preferred_element_type=F32)

    @pl.when(j == 0)
    def _():
        for s in range(n_slabs):
            rows = slice(s * FFN_SLAB, (s + 1) * FFN_SLAB)
            h = _rms(x_ref[rows, :], nw_ref[...]).astype(BF16)
            h_ref[rows, :] = h
            o_ref[rows, :] = swiglu_down(h)

    @pl.when(jnp.logical_and(j > 0, j < n_f - 1))
    def _():
        o_ref[...] += swiglu_down(h_ref[...])

    @pl.when(j == n_f - 1)
    def _():
        for s in range(n_slabs):
            rows = slice(s * FFN_SLAB, (s + 1) * FFN_SLAB)
            r = x_ref[rows, :] + 0.5 * (o_ref[rows, :] + swiglu_down(h_ref[rows, :]))
            if final:
                r = _rms(r, fw_ref[...])
            o_ref[rows, :] = r


def _ffn(x, nw, wg, wu, wd, fw, *, final, cast=None, mix_cast=None, tm=FFN_TM, tf=FFN_TF):
    t = x.shape[0]
    n_i, n_f = t // tm, D_FF // tf
    rb = D_MODEL // n_i
    in_specs = [
        pl.BlockSpec((tm, D_MODEL), lambda i, j: (i, 0)),
        pl.BlockSpec((1, D_MODEL), lambda i, j: (0, 0)),
        pl.BlockSpec((D_MODEL, tf), lambda i, j: (0, j)),
        pl.BlockSpec((D_MODEL, tf), lambda i, j: (0, j)),
        pl.BlockSpec((tf, D_MODEL), lambda i, j: (j, 0)),
        pl.BlockSpec((1, D_MODEL), lambda i, j: (0, 0)),
    ]
    out_specs = [pl.BlockSpec((tm, D_MODEL), lambda i, j: (i, 0))]
    out_shape = [jax.ShapeDtypeStruct((t, D_MODEL), F32)]
    cast_args = ()
    if cast is not None:
        cast_args, layer = cast
        assert rb * n_i == D_MODEL and rb % (2 * SUBLANE) == 0 and rb % LANE == 0
        in_specs += [
            pl.BlockSpec((None, rb, tf), lambda i, j: (layer, i, j)),
            pl.BlockSpec((None, rb, tf), lambda i, j: (layer, i, j)),
            pl.BlockSpec((None, tf, rb), lambda i, j: (layer, j, i)),
        ]
        out_specs += [
            pl.BlockSpec((rb, tf), lambda i, j: (i, j)),
            pl.BlockSpec((rb, tf), lambda i, j: (i, j)),
            pl.BlockSpec((tf, rb), lambda i, j: (j, i)),
        ]
        out_shape += [
            jax.ShapeDtypeStruct((D_MODEL, D_FF), BF16),
            jax.ShapeDtypeStruct((D_MODEL, D_FF), BF16),
            jax.ShapeDtypeStruct((D_FF, D_MODEL), BF16),
        ]
    slow_args = ()
    if mix_cast is not None:
        *slow_args, mix_layer = mix_cast
        assert rb * n_i == D_MODEL and rb % (2 * SUBLANE) == 0

        def cblock(j):
            return (j * FFN_CAST_BLOCKS) // n_f

        slow_in, slow_out = [], []
        for cols in (MAIN_COLS, D_MODEL):
            cb = cols // FFN_CAST_BLOCKS
            assert cb * FFN_CAST_BLOCKS == cols and cb % LANE == 0
            slow_in.append(pl.BlockSpec((None, rb, cb), lambda i, j: (mix_layer, i, cblock(j))))
            slow_out.append(pl.BlockSpec((rb, cb), lambda i, j: (i, cblock(j))))
            out_shape.append(jax.ShapeDtypeStruct((D_MODEL, cols), BF16))
        in_specs += slow_in
        out_specs += slow_out
    n_cast, n_slow = len(cast_args), len(slow_args)
    out = pl.pallas_call(
        functools.partial(_ffn_body, n_f=n_f, final=final, n_cast=n_cast, n_slow_cast=n_slow),
        grid=(n_i, n_f),
        in_specs=in_specs,
        out_specs=out_specs,
        out_shape=out_shape,
        scratch_shapes=[pltpu.VMEM((tm, D_MODEL), BF16)],
        compiler_params=pltpu.CompilerParams(
            dimension_semantics=("arbitrary", "arbitrary"), vmem_limit_bytes=VMEM_LIMIT),
        name="ffn_final" if final else "ffn",
    )(x, nw, wg, wu, wd, fw, *cast_args, *slow_args)
    return out[0], out[1:1 + n_cast], out[1 + n_cast:]


def _chunk_tril(tm):
    t = np.arange(tm)[:, None]
    i = np.arange(tm)[None, :]
    return ((i <= t) & (i // CHUNK == t // CHUNK)).astype(np.float32)


def _group_row(v, group, pick):
    rows, cols = v.shape
    g = v.reshape(rows // group, group, cols)[:, pick:pick + 1, :]
    return jnp.broadcast_to(g, (rows // group, group, cols)).reshape(rows, cols)


def _mix_body(x_ref, nw_ref, win_ref, wz_ref, cw_ref, gw2_ref, gb_ref, gn_ref, tri_ref, wo_ref,
              o_ref, s_ref, ubuf_ref, pc_ref, pqk_ref, pvg_ref, lap_ref, eb_ref, qs_ref, ks_ref,
              og_ref, y_ref, xprev_ref, b_ref, *, tm):
    i = pl.program_id(0)

    @pl.when(i == 0)
    def _():
        for ref in (s_ref, pvg_ref, eb_ref, qs_ref, ks_ref, y_ref, xprev_ref):
            ref[...] = jnp.zeros_like(ref)
        ubuf_ref[0:SUBLANE, :] = jnp.zeros((SUBLANE, D_CONV), F32)
        lap_ref[0:SUBLANE, :] = jnp.zeros((SUBLANE, D_QK), F32)
        lap_ref[SUBLANE + tm:2 * SUBLANE + tm, :] = jnp.zeros((SUBLANE, D_QK), F32)

    h = _rms(x_ref[...], nw_ref[...]).astype(BF16)
    row = lax.broadcasted_iota(jnp.int32, (CHUNK, CHUNK), 0)
    col = lax.broadcasted_iota(jnp.int32, (CHUNK, CHUNK), 1)
    rc = row ^ col

    def project(dst_ref, col0, blk):
        cs = slice(blk * MXU_N, (blk + 1) * MXU_N)
        dst_ref[:, cs] = jnp.dot(h, win_ref[:, col0 + blk * MXU_N:col0 + (blk + 1) * MXU_N],
                                 preferred_element_type=F32)

    def head_chunk(c, hd):
        rows = slice(c * CHUNK, (c + 1) * CHUNK)
        ks = slice(hd * HEAD_K, (hd + 1) * HEAD_K)
        vs = slice(hd * HEAD_V, (hd + 1) * HEAD_V)
        scores = jnp.where(rc == 0,
                           _dot_nt(qs_ref[N_LEVELS, rows, ks], ks_ref[N_LEVELS, rows, ks]), 0.0)
        for l in range(N_LEVELS):
            scores = jnp.where((rc >> l) == 1,
                               _dot_nt(qs_ref[l, rows, ks], ks_ref[l, rows, ks]), scores)
        vh = pvg_ref[rows, vs].astype(BF16)
        st = s_ref[hd]
        og_ref[rows, vs] = (jnp.dot(scores.astype(BF16), vh, preferred_element_type=F32)
                            + _dot_nt(qs_ref[N_LEVELS + 1, rows, ks], st.astype(BF16)))
        s_ref[hd] = (eb_ref[c * SUBLANE:c * SUBLANE + 1, ks] * st
                     + _dot_tn(vh, ks_ref[N_LEVELS + 1, rows, ks]))

    def wout_block(blk):
        cs = slice(blk * MXU_N, (blk + 1) * MXU_N)
        o_ref[:, cs] = xprev_ref[:, cs] + jnp.dot(y_ref[...], wo_ref[:, cs],
                                                  preferred_element_type=F32)

    def conv_block(blk):
        cs = slice(blk * MXU_N, (blk + 1) * MXU_N)
        u = pc_ref[:, 2 * D_CONV + blk * MXU_N:2 * D_CONV + (blk + 1) * MXU_N] * pc_ref[:, cs]
        ubuf_ref[SUBLANE:SUBLANE + tm, cs] = u
        cw = cw_ref[:, cs]
        conv = (cw[0:1, :] * ubuf_ref[SUBLANE - 2:SUBLANE - 2 + tm, cs]
                + cw[1:2, :] * ubuf_ref[SUBLANE - 1:SUBLANE - 1 + tm, cs]
                + cw[2:3, :] * u)
        y_ref[:, cs] = (pc_ref[:, D_CONV + blk * MXU_N:D_CONV + (blk + 1) * MXU_N] * conv).astype(BF16)
        ubuf_ref[0:SUBLANE, cs] = ubuf_ref[tm:tm + SUBLANE, cs]

    def prepare_rows(c):
        r0 = c * CHUNK
        rows = slice(r0, r0 + CHUNK)
        la = lap_ref[SUBLANE + r0:SUBLANE + r0 + CHUNK, :]
        la1 = lap_ref[SUBLANE + r0 + 1:SUBLANE + r0 + 1 + CHUNK, :]
        la2 = lap_ref[SUBLANE + r0 + 2:SUBLANE + r0 + 2 + CHUNK, :]
        b = b_ref[rows, :]
        q = pqk_ref[rows, 0:D_QK] * (HEAD_K ** -0.5)
        k = pqk_ref[rows, D_QK:2 * D_QK]
        rowi = lax.broadcasted_iota(jnp.int32, (CHUNK, D_QK), 0)
        r4 = rowi & 3
        for l in range(N_LEVELS):
            if l == 0:
                exponent = jnp.where((rowi & 1) == 0, la1, 0.0)
            elif l == 1:
                exponent = jnp.where(r4 == 0, la1 + la2,
                                     jnp.where(r4 == 1, la1, jnp.where(r4 == 3, la, 0.0)))
            else:
                exponent = -jnp.abs(b - _group_row(b, 2 << l, 1 << l))
            gl = jnp.exp(exponent)
            qs_ref[l, rows, :] = (q * gl).astype(BF16)
            ks_ref[l, rows, :] = (k * gl).astype(BF16)
        qs_ref[N_LEVELS, rows, :] = q.astype(BF16)
        ks_ref[N_LEVELS, rows, :] = k.astype(BF16)
        b_last = jnp.broadcast_to(b[CHUNK - 1:CHUNK, :], (CHUNK, D_QK))
        eb_ref[c * SUBLANE:(c + 1) * SUBLANE, :] = jnp.exp(b_last[0:SUBLANE, :])
        qs_ref[N_LEVELS + 1, rows, :] = (q * jnp.exp(b)).astype(BF16)
        ks_ref[N_LEVELS + 1, rows, :] = (k * jnp.exp(b_last - b)).astype(BF16)

    zl = jnp.dot(h, wz_ref[...], preferred_element_type=F32)
    zz = jnp.dot(zl.astype(BF16), gw2_ref[...], preferred_element_type=F32) + gb_ref[...]
    la = (jnp.minimum(zz, 0.0) - jnp.log(1.0 + jnp.exp(-jnp.abs(zz)))) * (1.0 / GATE_TAU)
    lap_ref[SUBLANE:SUBLANE + tm, :] = la
    la_hi = la.astype(BF16)
    r1 = la - la_hi.astype(F32)
    la_mid = r1.astype(BF16)
    la_lo = (r1 - la_mid.astype(F32)).astype(BF16)
    tri = tri_ref[...]
    b_ref[...] = (jnp.dot(tri, la_hi, preferred_element_type=F32)
                  + jnp.dot(tri, la_mid, preferred_element_type=F32)
                  + jnp.dot(tri, la_lo, preferred_element_type=F32))

    cq_blocks = ([(pqk_ref, OFF_Q, blk) for blk in range(2 * D_QK // MXU_N)]
                 + [(pc_ref, 0, blk) for blk in range(OFF_Q // MXU_N)])
    for c in range(tm // CHUNK):
        for hd in range(N_HEADS):
            project(*cq_blocks[c * N_HEADS + hd])
            head_chunk(c, hd)
        prepare_rows(c)

    for hd in range(N_HEADS):
        vs = slice(hd * HEAD_V, (hd + 1) * HEAD_V)
        gate = pvg_ref[:, D_GLA + hd * HEAD_V:D_GLA + (hd + 1) * HEAD_V]
        y_ref[:, D_CONV + hd * HEAD_V:D_CONV + (hd + 1) * HEAD_V] = (
            _rms(og_ref[:, vs], gn_ref[...]) * (gate * _sigmoid(gate))).astype(BF16)

    for blk in range(D_MODEL // MXU_N):
        wout_block(blk)
    xprev_ref[...] = x_ref[...]
    for blk in range(2 * D_GLA // MXU_N):
        project(pvg_ref, OFF_V, blk)
        if blk < D_CONV // MXU_N:
            conv_block(blk)


def _mix(x, nw, w_in, w_z, conv_w, gw2, gate_b, gla_norm, tri, w_out, *, tm):
    t = x.shape[0]
    n = t // tm

    def resident(shape):
        return pl.BlockSpec(shape, lambda i: (0,) * len(shape), pipeline_mode=pl.Buffered(1))

    return pl.pallas_call(
        functools.partial(_mix_body, tm=tm),
        grid=(n + 1,),
        in_specs=[
            pl.BlockSpec((tm, D_MODEL), lambda i: (jnp.minimum(i, n - 1), 0)),
            resident((1, D_MODEL)),
            resident((D_MODEL, MAIN_COLS)),
            resident((D_MODEL, LANE)),
            resident((CONV_WIDTH, D_CONV)),
            resident((LANE, D_QK)),
            resident((1, D_QK)),
            resident((1, HEAD_V)),
            resident((tm, tm)),
            resident((D_MODEL, D_MODEL)),
        ],
        out_specs=pl.BlockSpec((tm, D_MODEL), lambda i: (jnp.maximum(i - 1, 0), 0)),
        out_shape=jax.ShapeDtypeStruct((t, D_MODEL), F32),
        scratch_shapes=[
            pltpu.VMEM((N_HEADS, HEAD_V, HEAD_K), F32),
            pltpu.VMEM((tm + SUBLANE, D_CONV), F32),
            pltpu.VMEM((tm, 3 * D_CONV), F32),
            pltpu.VMEM((tm, 2 * D_QK), F32),
            pltpu.VMEM((tm, 2 * D_GLA), F32),
            pltpu.VMEM((tm + 2 * SUBLANE, D_QK), F32),
            pltpu.VMEM((tm // CHUNK * SUBLANE, D_QK), F32),
            pltpu.VMEM((N_LEVELS + 2, tm, D_QK), BF16),
            pltpu.VMEM((N_LEVELS + 2, tm, D_QK), BF16),
            pltpu.VMEM((tm, D_GLA), F32),
            pltpu.VMEM((tm, D_MODEL), BF16),
            pltpu.VMEM((tm, D_MODEL), F32),
            pltpu.VMEM((tm, D_QK), F32),
        ],
        compiler_params=pltpu.CompilerParams(
            dimension_semantics=("arbitrary",), vmem_limit_bytes=VMEM_LIMIT),
        name="mix",
    )(x, nw, w_in, w_z, conv_w, gw2, gate_b, gla_norm, tri, w_out)


def kernel(x, ffn1_norm, ffn1_w_gate, ffn1_w_up, ffn1_w_down, mix_norm, w_in, conv_w, gate_w2,
           gate_b, gla_norm, w_out, ffn2_norm, ffn2_w_gate, ffn2_w_up, ffn2_w_down, final_norm):
    b, t, d = x.shape
    depth = w_in.shape[0]
    xs = x.reshape(b * t, d)
    tri = jnp.asarray(_chunk_tril(MIX_TM), BF16)
    fw = final_norm.reshape(1, d)
    ffn1 = (ffn1_w_gate, ffn1_w_up, ffn1_w_down)
    ffn2 = (ffn2_w_gate, ffn2_w_up, ffn2_w_down)
    weights = [w[0].astype(BF16) for w in ffn1]
    w_tail = lax.optimization_barrier(w_in[:, :, MAIN_COLS:])
    w_tail = jnp.pad(w_tail, ((0, 0), (0, 0), (0, LANE - GATE_RANK))).astype(BF16)
    for l in range(depth):
        last = l == depth - 1
        xs, weights, (w_in_bf, w_out_bf) = _ffn(xs, ffn1_norm[l].reshape(1, d), *weights, fw, final=False,
                                                cast=(ffn2, l), mix_cast=(w_in, w_out, l))
        gw2 = jnp.pad(gate_w2[l], ((0, LANE - GATE_RANK), (0, 0))).astype(BF16)
        xs = _mix(xs, mix_norm[l].reshape(1, d), w_in_bf, w_tail[l], conv_w[l], gw2,
                  gate_b[l].reshape(1, D_QK), gla_norm[l].reshape(1, HEAD_V), tri, w_out_bf, tm=MIX_TM)
        xs, weights, _ = _ffn(xs, ffn2_norm[l].reshape(1, d), *weights, fw, final=last,
                              cast=None if last else (ffn1, l + 1))
    return xs.reshape(b, t, d)
```

```python
import functools

import jax
import jax.numpy as jnp
import numpy as np
from jax import lax
from jax.experimental import pallas as pl
from jax.experimental.pallas import tpu as pltpu

D_MODEL = 2048
CHUNK = 64
D_CONV = 1024
D_GLA = 1024
N_HEADS = 4
HEAD_V = 256
HEAD_K = 128
D_QK = 512
GATE_RANK = 16
GATE_TAU = 16.0
CONV_WIDTH = 3
D_FF = 5632
EPS = 1e-6
MAIN_COLS = 3 * D_CONV + 2 * D_QK + 2 * D_GLA
LANE = 128
SUBLANE = 8
N_LEVELS = 6
MXU_N = 256
VMEM_LIMIT = 60 * 1024 * 1024
FFN_TM = 1024
FFN_TF = 512
FFN_SLAB = 256
FFN_CAST_BLOCKS = 4
MIX_TM = 256

OFF_Q = 3 * D_CONV
OFF_K = OFF_Q + D_QK
OFF_V = OFF_K + D_QK
OFF_G = OFF_V + D_GLA

F32 = jnp.float32
BF16 = jnp.bfloat16


def _rms(x, w):
    ms = jnp.mean(x * x, axis=-1, keepdims=True)
    return x * lax.rsqrt(ms + EPS) * w


def _sigmoid(x):
    return 1.0 / (1.0 + jnp.exp(-x))


def _dot_nt(a, b):
    return lax.dot_general(a, b, (((1,), (1,)), ((), ())), preferred_element_type=F32)


def _dot_tn(a, b):
    return lax.dot_general(a, b, (((0,), (0,)), ((), ())), preferred_element_type=F32)


def _ffn_body(x_ref, nw_ref, wg_ref, wu_ref, wd_ref, fw_ref, *refs, n_f, final, n_cast, n_slow_cast):
    n_in = n_cast + n_slow_cast
    cast_in, slow_in = refs[:n_cast], refs[n_cast:n_in]
    o_ref = refs[n_in]
    cast_out, slow_out = refs[n_in + 1:n_in + 1 + n_cast], refs[n_in + 1 + n_cast:2 * n_in + 1]
    h_ref = refs[2 * n_in + 1]
    j = pl.program_id(1)
    n_slabs = x_ref.shape[0] // FFN_SLAB

    for src_ref, dst_ref in zip(cast_in, cast_out, strict=True):
        dst_ref[...] = src_ref[...].astype(BF16)

    if n_slow_cast:
        @pl.when((j * FFN_CAST_BLOCKS) % n_f < FFN_CAST_BLOCKS)
        def _():
            w_in_t_ref, w_out_ref, tail_t_ref = slow_in
            slow_out[0][...] = w_in_t_ref[...].T.astype(BF16)
            slow_out[1][...] = w_out_ref[...].astype(BF16)
            tail_t = jnp.concatenate(
                [tail_t_ref[...], jnp.zeros((LANE - GATE_RANK, D_MODEL), F32)], axis=0)
            slow_out[2][...] = tail_t.T.astype(BF16)

    def swiglu_down(h):
        g = jnp.dot(h, wg_ref[...], preferred_element_type=F32)
        u = jnp.dot(h, wu_ref[...], preferred_element_type=F32)
        a = (g * _sigmoid(g) * u).astype(BF16)
        return jnp.dot(a, wd_ref[...], preferred_element_type=F32)

    @pl.when(j == 0)
    def _():
        for s in range(n_slabs):
            rows = slice(s * FFN_SLAB, (s + 1) * FFN_SLAB)
            h = _rms(x_ref[rows, :], nw_ref[...]).astype(BF16)
            h_ref[rows, :] = h
            o_ref[rows, :] = swiglu_down(h)

    @pl.when(jnp.logical_and(j > 0, j < n_f - 1))
    def _():
        o_ref[...] += swiglu_down(h_ref[...])

    @pl.when(j == n_f - 1)
    def _():
        for s in range(n_slabs):
            rows = slice(s * FFN_SLAB, (s + 1) * FFN_SLAB)
            r = x_ref[rows, :] + 0.5 * (o_ref[rows, :] + swiglu_down(h_ref[rows, :]))
            if final:
                r = _rms(r, fw_ref[...])
            o_ref[rows, :] = r


def _ffn(x, nw, wg, wu, wd, fw, *, final, cast=None, mix_cast=None, tm=FFN_TM, tf=FFN_TF):
    t = x.shape[0]
    n_i, n_f = t // tm, D_FF // tf
    rb = D_MODEL // n_i
    in_specs = [
        pl.BlockSpec((tm, D_MODEL), lambda i, j: (i, 0)),
        pl.BlockSpec((1, D_MODEL), lambda i, j: (0, 0)),
        pl.BlockSpec((D_MODEL, tf), lambda i, j: (0, j)),
        pl.BlockSpec((D_MODEL, tf), lambda i, j: (0, j)),
        pl.BlockSpec((tf, D_MODEL), lambda i, j: (j, 0)),
        pl.BlockSpec((1, D_MODEL), lambda i, j: (0, 0)),
    ]
    out_specs = [pl.BlockSpec((tm, D_MODEL), lambda i, j: (i, 0))]
    out_shape = [jax.ShapeDtypeStruct((t, D_MODEL), F32)]
    cast_args = ()
    if cast is not None:
        cast_args, layer = cast
        assert rb * n_i == D_MODEL and rb % (2 * SUBLANE) == 0 and rb % LANE == 0
        in_specs += [
            pl.BlockSpec((None, rb, tf), lambda i, j: (layer, i, j)),
            pl.BlockSpec((None, rb, tf), lambda i, j: (layer, i, j)),
            pl.BlockSpec((None, tf, rb), lambda i, j: (layer, j, i)),
        ]
        out_specs += [
            pl.BlockSpec((rb, tf), lambda i, j: (i, j)),
            pl.BlockSpec((rb, tf), lambda i, j: (i, j)),
            pl.BlockSpec((tf, rb), lambda i, j: (j, i)),
        ]
        out_shape += [
            jax.ShapeDtypeStruct((D_MODEL, D_FF), BF16),
            jax.ShapeDtypeStruct((D_MODEL, D_FF), BF16),
            jax.ShapeDtypeStruct((D_FF, D_MODEL), BF16),
        ]
    slow_args = ()
    if mix_cast is not None:
        *slow_args, mix_layer = mix_cast
        assert rb * n_i == D_MODEL and rb % (2 * SUBLANE) == 0

        def cblock(j):
            return (j * FFN_CAST_BLOCKS) // n_f

        slow_in, slow_out = [], []
        for cols, transposed in ((MAIN_COLS, True), (D_MODEL, False)):
            cb = cols // FFN_CAST_BLOCKS
            assert cb * FFN_CAST_BLOCKS == cols and cb % LANE == 0 and rb % LANE == 0
            if transposed:
                slow_in.append(pl.BlockSpec((None, cb, rb), lambda i, j: (mix_layer, cblock(j), i)))
            else:
                slow_in.append(pl.BlockSpec((None, rb, cb), lambda i, j: (mix_layer, i, cblock(j))))
            slow_out.append(pl.BlockSpec((rb, cb), lambda i, j: (i, cblock(j))))
            out_shape.append(jax.ShapeDtypeStruct((D_MODEL, cols), BF16))
        slow_args = [slow_args[0], slow_args[1], slow_args[0]]
        slow_in.append(pl.BlockSpec((None, GATE_RANK, D_MODEL),
                                    lambda i, j: (mix_layer, MAIN_COLS // GATE_RANK, 0)))
        slow_out.append(pl.BlockSpec((D_MODEL, LANE), lambda i, j: (0, 0)))
        out_shape.append(jax.ShapeDtypeStruct((D_MODEL, LANE), BF16))
        in_specs += slow_in
        out_specs += slow_out
    n_cast, n_slow = len(cast_args), len(slow_args)
    out = pl.pallas_call(
        functools.partial(_ffn_body, n_f=n_f, final=final, n_cast=n_cast, n_slow_cast=n_slow),
        grid=(n_i, n_f),
        in_specs=in_specs,
        out_specs=out_specs,
        out_shape=out_shape,
        scratch_shapes=[pltpu.VMEM((tm, D_MODEL), BF16)],
        compiler_params=pltpu.CompilerParams(
            dimension_semantics=("arbitrary", "arbitrary"), vmem_limit_bytes=VMEM_LIMIT),
        name="ffn_final" if final else "ffn",
    )(x, nw, wg, wu, wd, fw, *cast_args, *slow_args)
    return out[0], out[1:1 + n_cast], out[1 + n_cast:]


def _chunk_tril(tm):
    t = np.arange(tm)[:, None]
    i = np.arange(tm)[None, :]
    return ((i <= t) & (i // CHUNK == t // CHUNK)).astype(np.float32)


def _group_row(v, group, pick):
    rows, cols = v.shape
    g = v.reshape(rows // group, group, cols)[:, pick:pick + 1, :]
    return jnp.broadcast_to(g, (rows // group, group, cols)).reshape(rows, cols)


def _mix_body(x_ref, nw_ref, win_ref, wz_ref, cw_ref, gw2_ref, gb_ref, gn_ref, tri_ref, wo_ref,
              o_ref, s_ref, ubuf_ref, pc_ref, pqk_ref, pvg_ref, lap_ref, eb_ref, qs_ref, ks_ref,
              og_ref, y_ref, xprev_ref, b_ref, *, tm):
    i = pl.program_id(0)

    @pl.when(i == 0)
    def _():
        for ref in (s_ref, pvg_ref, eb_ref, qs_ref, ks_ref, y_ref, xprev_ref):
            ref[...] = jnp.zeros_like(ref)
        ubuf_ref[0:SUBLANE, :] = jnp.zeros((SUBLANE, D_CONV), F32)
        lap_ref[0:SUBLANE, :] = jnp.zeros((SUBLANE, D_QK), F32)
        lap_ref[SUBLANE + tm:2 * SUBLANE + tm, :] = jnp.zeros((SUBLANE, D_QK), F32)

    h = _rms(x_ref[...], nw_ref[...]).astype(BF16)
    row = lax.broadcasted_iota(jnp.int32, (CHUNK, CHUNK), 0)
    col = lax.broadcasted_iota(jnp.int32, (CHUNK, CHUNK), 1)
    rc = row ^ col

    def project(dst_ref, col0, blk):
        cs = slice(blk * MXU_N, (blk + 1) * MXU_N)
        dst_ref[:, cs] = jnp.dot(h, win_ref[:, col0 + blk * MXU_N:col0 + (blk + 1) * MXU_N],
                                 preferred_element_type=F32)

    def head_chunk(c, hd):
        rows = slice(c * CHUNK, (c + 1) * CHUNK)
        ks = slice(hd * HEAD_K, (hd + 1) * HEAD_K)
        vs = slice(hd * HEAD_V, (hd + 1) * HEAD_V)
        scores = jnp.where(rc == 0,
                           _dot_nt(qs_ref[N_LEVELS, rows, ks], ks_ref[N_LEVELS, rows, ks]), 0.0)
        for l in range(N_LEVELS):
            scores = jnp.where((rc >> l) == 1,
                               _dot_nt(qs_ref[l, rows, ks], ks_ref[l, rows, ks]), scores)
        vh = pvg_ref[rows, vs].astype(BF16)
        st = s_ref[hd]
        og_ref[rows, vs] = (jnp.dot(scores.astype(BF16), vh, preferred_element_type=F32)
                            + _dot_nt(qs_ref[N_LEVELS + 1, rows, ks], st.astype(BF16)))
        s_ref[hd] = (eb_ref[c * SUBLANE:c * SUBLANE + 1, ks] * st
                     + _dot_tn(vh, ks_ref[N_LEVELS + 1, rows, ks]))

    def wout_block(blk):
        cs = slice(blk * MXU_N, (blk + 1) * MXU_N)
        o_ref[:, cs] = xprev_ref[:, cs] + jnp.dot(y_ref[...], wo_ref[:, cs],
                                                  preferred_element_type=F32)

    def conv_block(blk):
        cs = slice(blk * MXU_N, (blk + 1) * MXU_N)
        u = pc_ref[:, 2 * D_CONV + blk * MXU_N:2 * D_CONV + (blk + 1) * MXU_N] * pc_ref[:, cs]
        ubuf_ref[SUBLANE:SUBLANE + tm, cs] = u
        cw = cw_ref[:, cs]
        conv = (cw[0:1, :] * ubuf_ref[SUBLANE - 2:SUBLANE - 2 + tm, cs]
                + cw[1:2, :] * ubuf_ref[SUBLANE - 1:SUBLANE - 1 + tm, cs]
                + cw[2:3, :] * u)
        y_ref[:, cs] = (pc_ref[:, D_CONV + blk * MXU_N:D_CONV + (blk + 1) * MXU_N] * conv).astype(BF16)
        ubuf_ref[0:SUBLANE, cs] = ubuf_ref[tm:tm + SUBLANE, cs]

    def prepare_rows(c):
        r0 = c * CHUNK
        rows = slice(r0, r0 + CHUNK)
        la = lap_ref[SUBLANE + r0:SUBLANE + r0 + CHUNK, :]
        la1 = lap_ref[SUBLANE + r0 + 1:SUBLANE + r0 + 1 + CHUNK, :]
        la2 = lap_ref[SUBLANE + r0 + 2:SUBLANE + r0 + 2 + CHUNK, :]
        b = b_ref[rows, :]
        q = pqk_ref[rows, 0:D_QK] * (HEAD_K ** -0.5)
        k = pqk_ref[rows, D_QK:2 * D_QK]
        rowi = lax.broadcasted_iota(jnp.int32, (CHUNK, D_QK), 0)
        r4 = rowi & 3
        for l in range(N_LEVELS):
            if l == 0:
                exponent = jnp.where((rowi & 1) == 0, la1, 0.0)
            elif l == 1:
                exponent = jnp.where(r4 == 0, la1 + la2,
                                     jnp.where(r4 == 1, la1, jnp.where(r4 == 3, la, 0.0)))
            else:
                exponent = -jnp.abs(b - _group_row(b, 2 << l, 1 << l))
            gl = jnp.exp(exponent)
            qs_ref[l, rows, :] = (q * gl).astype(BF16)
            ks_ref[l, rows, :] = (k * gl).astype(BF16)
        qs_ref[N_LEVELS, rows, :] = q.astype(BF16)
        ks_ref[N_LEVELS, rows, :] = k.astype(BF16)
        b_last = jnp.broadcast_to(b[CHUNK - 1:CHUNK, :], (CHUNK, D_QK))
        eb_ref[c * SUBLANE:(c + 1) * SUBLANE, :] = jnp.exp(b_last[0:SUBLANE, :])
        qs_ref[N_LEVELS + 1, rows, :] = (q * jnp.exp(b)).astype(BF16)
        ks_ref[N_LEVELS + 1, rows, :] = (k * jnp.exp(b_last - b)).astype(BF16)

    zl = jnp.dot(h, wz_ref[...], preferred_element_type=F32)
    zz = jnp.dot(zl.astype(BF16), gw2_ref[...], preferred_element_type=F32) + gb_ref[...]
    la = (jnp.minimum(zz, 0.0) - jnp.log(1.0 + jnp.exp(-jnp.abs(zz)))) * (1.0 / GATE_TAU)
    lap_ref[SUBLANE:SUBLANE + tm, :] = la
    la_hi = la.astype(BF16)
    r1 = la - la_hi.astype(F32)
    la_mid = r1.astype(BF16)
    la_lo = (r1 - la_mid.astype(F32)).astype(BF16)
    tri = tri_ref[...]
    b_ref[...] = (jnp.dot(tri, la_hi, preferred_element_type=F32)
                  + jnp.dot(tri, la_mid, preferred_element_type=F32)
                  + jnp.dot(tri, la_lo, preferred_element_type=F32))

    cq_blocks = ([(pqk_ref, OFF_Q, blk) for blk in range(2 * D_QK // MXU_N)]
                 + [(pc_ref, 0, blk) for blk in range(OFF_Q // MXU_N)])
    for c in range(tm // CHUNK):
        for hd in range(N_HEADS):
            project(*cq_blocks[c * N_HEADS + hd])
            head_chunk(c, hd)
        prepare_rows(c)

    for hd in range(N_HEADS):
        vs = slice(hd * HEAD_V, (hd + 1) * HEAD_V)
        gate = pvg_ref[:, D_GLA + hd * HEAD_V:D_GLA + (hd + 1) * HEAD_V]
        y_ref[:, D_CONV + hd * HEAD_V:D_CONV + (hd + 1) * HEAD_V] = (
            _rms(og_ref[:, vs], gn_ref[...]) * (gate * _sigmoid(gate))).astype(BF16)

    for blk in range(D_MODEL // MXU_N):
        wout_block(blk)
    xprev_ref[...] = x_ref[...]
    for blk in range(2 * D_GLA // MXU_N):
        project(pvg_ref, OFF_V, blk)
        if blk < D_CONV // MXU_N:
            conv_block(blk)


def _mix(x, nw, w_in, w_z, conv_w, gw2, gate_b, gla_norm, tri, w_out, *, tm):
    t = x.shape[0]
    n = t // tm

    def resident(shape):
        return pl.BlockSpec(shape, lambda i: (0,) * len(shape), pipeline_mode=pl.Buffered(1))

    return pl.pallas_call(
        functools.partial(_mix_body, tm=tm),
        grid=(n + 1,),
        in_specs=[
            pl.BlockSpec((tm, D_MODEL), lambda i: (jnp.minimum(i, n - 1), 0)),
            resident((1, D_MODEL)),
            resident((D_MODEL, MAIN_COLS)),
            resident((D_MODEL, LANE)),
            resident((CONV_WIDTH, D_CONV)),
            resident((LANE, D_QK)),
            resident((1, D_QK)),
            resident((1, HEAD_V)),
            resident((tm, tm)),
            resident((D_MODEL, D_MODEL)),
        ],
        out_specs=pl.BlockSpec((tm, D_MODEL), lambda i: (jnp.maximum(i - 1, 0), 0)),
        out_shape=jax.ShapeDtypeStruct((t, D_MODEL), F32),
        scratch_shapes=[
            pltpu.VMEM((N_HEADS, HEAD_V, HEAD_K), F32),
            pltpu.VMEM((tm + SUBLANE, D_CONV), F32),
            pltpu.VMEM((tm, 3 * D_CONV), F32),
            pltpu.VMEM((tm, 2 * D_QK), F32),
            pltpu.VMEM((tm, 2 * D_GLA), F32),
            pltpu.VMEM((tm + 2 * SUBLANE, D_QK), F32),
            pltpu.VMEM((tm // CHUNK * SUBLANE, D_QK), F32),
            pltpu.VMEM((N_LEVELS + 2, tm, D_QK), BF16),
            pltpu.VMEM((N_LEVELS + 2, tm, D_QK), BF16),
            pltpu.VMEM((tm, D_GLA), F32),
            pltpu.VMEM((tm, D_MODEL), BF16),
            pltpu.VMEM((tm, D_MODEL), F32),
            pltpu.VMEM((tm, D_QK), F32),
        ],
        compiler_params=pltpu.CompilerParams(
            dimension_semantics=("arbitrary",), vmem_limit_bytes=VMEM_LIMIT),
        name="mix",
    )(x, nw, w_in, w_z, conv_w, gw2, gate_b, gla_norm, tri, w_out)


def kernel(x, ffn1_norm, ffn1_w_gate, ffn1_w_up, ffn1_w_down, mix_norm, w_in, conv_w, gate_w2,
           gate_b, gla_norm, w_out, ffn2_norm, ffn2_w_gate, ffn2_w_up, ffn2_w_down, final_norm):
    b, t, d = x.shape
    depth = w_in.shape[0]
    xs = x.reshape(b * t, d)
    tri = jnp.asarray(_chunk_tril(MIX_TM), BF16)
    fw = final_norm.reshape(1, d)
    ffn1 = (ffn1_w_gate, ffn1_w_up, ffn1_w_down)
    ffn2 = (ffn2_w_gate, ffn2_w_up, ffn2_w_down)
    weights = [w[0].astype(BF16) for w in ffn1]
    w_in_t = jnp.swapaxes(w_in, 1, 2)
    for l in range(depth):
        last = l == depth - 1
        xs, weights, (w_in_bf, w_out_bf, w_tail_t) = _ffn(xs, ffn1_norm[l].reshape(1, d), *weights, fw, final=False,
                                                cast=(ffn2, l), mix_cast=(w_in_t, w_out, l))
        gw2 = jnp.pad(gate_w2[l], ((0, LANE - GATE_RANK), (0, 0))).astype(BF16)
        xs = _mix(xs, mix_norm[l].reshape(1, d), w_in_bf, w_tail_t, conv_w[l], gw2,
                  gate_b[l].reshape(1, D_QK), gla_norm[l].reshape(1, HEAD_V), tri, w_out_bf, tm=MIX_TM)
        xs, weights, _ = _ffn(xs, ffn2_norm[l].reshape(1, d), *weights, fw, final=last,
                              cast=None if last else (ffn1, l + 1))
    return xs.reshape(b, t, d)
```

```python
import functools

import jax
import jax.numpy as jnp
import numpy as np
from jax import lax
from jax.experimental import pallas as pl
from jax.experimental.pallas import tpu as pltpu

D_MODEL = 2048
CHUNK = 64
D_CONV = 1024
D_GLA = 1024
N_HEADS = 4
HEAD_V = 256
HEAD_K = 128
D_QK = 512
GATE_RANK = 16
GATE_TAU = 16.0
CONV_WIDTH = 3
D_FF = 5632
EPS = 1e-6
MAIN_COLS = 3 * D_CONV + 2 * D_QK + 2 * D_GLA
LANE = 128
SUBLANE = 8
N_LEVELS = 6
MXU_N = 256
VMEM_LIMIT = 60 * 1024 * 1024
FFN_TM = 1024
FFN_TF = 512
FFN_FIRST_SLAB = 512
FFN_LAST_SLAB = 256
FFN_RES_SCALE = 0.5
FFN_CAST_BLOCKS = 4
MIX_TM = 256

OFF_Q = 3 * D_CONV
OFF_V = OFF_Q + 2 * D_QK

F32 = jnp.float32
BF16 = jnp.bfloat16


def _rms(x, w):
    ms = jnp.mean(x * x, axis=-1, keepdims=True)
    return x * lax.rsqrt(ms + EPS) * w


def _sigmoid(x):
    return 1.0 / (1.0 + jnp.exp(-x))


def _dot_nt(a, b):
    return lax.dot_general(a, b, (((1,), (1,)), ((), ())), preferred_element_type=F32)


def _dot_tn(a, b):
    return lax.dot_general(a, b, (((0,), (0,)), ((), ())), preferred_element_type=F32)


def _ffn_body(x_ref, nw_ref, wg_ref, wu_ref, wd_ref, fw_ref, *refs, n_f, final, n_cast, n_slow_cast):
    n_in = n_cast + n_slow_cast
    cast_in, slow_in = refs[:n_cast], refs[n_cast:n_in]
    o_ref = refs[n_in]
    cast_out, slow_out = refs[n_in + 1:n_in + 1 + n_cast], refs[n_in + 1 + n_cast:2 * n_in + 1]
    h_ref = refs[2 * n_in + 1]
    j = pl.program_id(1)
    tm = x_ref.shape[0]

    for src_ref, dst_ref in zip(cast_in, cast_out, strict=True):
        dst_ref[...] = src_ref[...].astype(BF16)

    if n_slow_cast:
        @pl.when((j * FFN_CAST_BLOCKS) % n_f < FFN_CAST_BLOCKS)
        def _():
            w_in_t_ref, w_out_ref, tail_t_ref = slow_in
            slow_out[0][...] = w_in_t_ref[...].T.astype(BF16)
            slow_out[1][...] = w_out_ref[...].astype(BF16)
            tail_t = jnp.concatenate(
                [tail_t_ref[...], jnp.zeros((LANE - GATE_RANK, D_MODEL), F32)], axis=0)
            slow_out[2][...] = tail_t.T.astype(BF16)

    def swiglu_down(h):
        g = jnp.dot(h, wg_ref[...], preferred_element_type=F32)
        u = jnp.dot(h, wu_ref[...], preferred_element_type=F32)
        a = (g * _sigmoid(g) * u).astype(BF16)
        return jnp.dot(a, wd_ref[...], preferred_element_type=F32)

    @pl.when(j == 0)
    def _():
        for s in range(tm // FFN_FIRST_SLAB):
            rows = slice(s * FFN_FIRST_SLAB, (s + 1) * FFN_FIRST_SLAB)
            h = _rms(x_ref[rows, :], nw_ref[...]).astype(BF16)
            h_ref[rows, :] = h
            o_ref[rows, :] = swiglu_down(h)

    @pl.when(jnp.logical_and(j > 0, j < n_f - 1))
    def _():
        o_ref[...] += swiglu_down(h_ref[...])

    @pl.when(j == n_f - 1)
    def _():
        for s in range(tm // FFN_LAST_SLAB):
            rows = slice(s * FFN_LAST_SLAB, (s + 1) * FFN_LAST_SLAB)
            r = x_ref[rows, :] + FFN_RES_SCALE * (o_ref[rows, :] + swiglu_down(h_ref[rows, :]))
            if final:
                r = _rms(r, fw_ref[...])
            o_ref[rows, :] = r


def _ffn(x, nw, wg, wu, wd, fw, *, final, cast=None, mix_cast=None, tm=FFN_TM, tf=FFN_TF):
    t = x.shape[0]
    n_i, n_f = t // tm, D_FF // tf
    rb = D_MODEL // n_i
    in_specs = [
        pl.BlockSpec((tm, D_MODEL), lambda i, j: (i, 0)),
        pl.BlockSpec((1, D_MODEL), lambda i, j: (0, 0)),
        pl.BlockSpec((D_MODEL, tf), lambda i, j: (0, j)),
        pl.BlockSpec((D_MODEL, tf), lambda i, j: (0, j)),
        pl.BlockSpec((tf, D_MODEL), lambda i, j: (j, 0)),
        pl.BlockSpec((1, D_MODEL), lambda i, j: (0, 0)),
    ]
    out_specs = [pl.BlockSpec((tm, D_MODEL), lambda i, j: (i, 0))]
    out_shape = [jax.ShapeDtypeStruct((t, D_MODEL), F32)]
    cast_args = ()
    if cast is not None:
        cast_args, layer = cast
        assert rb * n_i == D_MODEL and rb % (2 * SUBLANE) == 0 and rb % LANE == 0
        in_specs += [
            pl.BlockSpec((None, rb, tf), lambda i, j: (layer, i, j)),
            pl.BlockSpec((None, rb, tf), lambda i, j: (layer, i, j)),
            pl.BlockSpec((None, tf, rb), lambda i, j: (layer, j, i)),
        ]
        out_specs += [
            pl.BlockSpec((rb, tf), lambda i, j: (i, j)),
            pl.BlockSpec((rb, tf), lambda i, j: (i, j)),
            pl.BlockSpec((tf, rb), lambda i, j: (j, i)),
        ]
        out_shape += [
            jax.ShapeDtypeStruct((D_MODEL, D_FF), BF16),
            jax.ShapeDtypeStruct((D_MODEL, D_FF), BF16),
            jax.ShapeDtypeStruct((D_FF, D_MODEL), BF16),
        ]
    slow_args = ()
    if mix_cast is not None:
        *slow_args, mix_layer = mix_cast
        assert rb * n_i == D_MODEL and rb % (2 * SUBLANE) == 0

        def cblock(j):
            return (j * FFN_CAST_BLOCKS) // n_f

        slow_in, slow_out = [], []
        for cols, transposed in ((MAIN_COLS, True), (D_MODEL, False)):
            cb = cols // FFN_CAST_BLOCKS
            assert cb * FFN_CAST_BLOCKS == cols and cb % LANE == 0 and rb % LANE == 0
            if transposed:
                slow_in.append(pl.BlockSpec((None, cb, rb), lambda i, j: (mix_layer, cblock(j), i)))
            else:
                slow_in.append(pl.BlockSpec((None, rb, cb), lambda i, j: (mix_layer, i, cblock(j))))
            slow_out.append(pl.BlockSpec((rb, cb), lambda i, j: (i, cblock(j))))
            out_shape.append(jax.ShapeDtypeStruct((D_MODEL, cols), BF16))
        slow_args = [slow_args[0], slow_args[1], slow_args[0]]
        slow_in.append(pl.BlockSpec((None, GATE_RANK, D_MODEL),
                                    lambda i, j: (mix_layer, MAIN_COLS // GATE_RANK, 0)))
        slow_out.append(pl.BlockSpec((D_MODEL, LANE), lambda i, j: (0, 0)))
        out_shape.append(jax.ShapeDtypeStruct((D_MODEL, LANE), BF16))
        in_specs += slow_in
        out_specs += slow_out
    n_cast, n_slow = len(cast_args), len(slow_args)
    out = pl.pallas_call(
        functools.partial(_ffn_body, n_f=n_f, final=final, n_cast=n_cast, n_slow_cast=n_slow),
        grid=(n_i, n_f),
        in_specs=in_specs,
        out_specs=out_specs,
        out_shape=out_shape,
        scratch_shapes=[pltpu.VMEM((tm, D_MODEL), BF16)],
        compiler_params=pltpu.CompilerParams(
            dimension_semantics=("arbitrary", "arbitrary"), vmem_limit_bytes=VMEM_LIMIT),
        name="ffn_final" if final else "ffn",
    )(x, nw, wg, wu, wd, fw, *cast_args, *slow_args)
    return out[0], out[1:1 + n_cast], out[1 + n_cast:]


def _chunk_tril(tm):
    t = np.arange(tm)[:, None]
    i = np.arange(tm)[None, :]
    return ((i <= t) & (i // CHUNK == t // CHUNK)).astype(np.float32)


def _group_row(v, group, pick):
    rows, cols = v.shape
    g = v.reshape(rows // group, group, cols)[:, pick:pick + 1, :]
    return jnp.broadcast_to(g, (rows // group, group, cols)).reshape(rows, cols)


def _mix_body(x_ref, nw_ref, win_ref, wz_ref, cw_ref, gw2_ref, gb_ref, gn_ref, tri_ref, wo_ref,
              o_ref, s_ref, ubuf_ref, pc_ref, pqk_ref, pvg_ref, lap_ref, eb_ref, qs_ref, ks_ref,
              og_ref, y_ref, xprev_ref, b_ref, *, tm):
    i = pl.program_id(0)

    @pl.when(i == 0)
    def _():
        for ref in (s_ref, pvg_ref, eb_ref, qs_ref, ks_ref, y_ref, xprev_ref):
            ref[...] = jnp.zeros_like(ref)
        ubuf_ref[0:SUBLANE, :] = jnp.zeros((SUBLANE, D_CONV), F32)
        lap_ref[0:SUBLANE, :] = jnp.zeros((SUBLANE, D_QK), F32)
        lap_ref[SUBLANE + tm:2 * SUBLANE + tm, :] = jnp.zeros((SUBLANE, D_QK), F32)

    h = _rms(x_ref[...], nw_ref[...]).astype(BF16)
    row = lax.broadcasted_iota(jnp.int32, (CHUNK, CHUNK), 0)
    col = lax.broadcasted_iota(jnp.int32, (CHUNK, CHUNK), 1)
    rc = row ^ col

    def project(dst_ref, col0, blk):
        cs = slice(blk * MXU_N, (blk + 1) * MXU_N)
        dst_ref[:, cs] = jnp.dot(h, win_ref[:, col0 + blk * MXU_N:col0 + (blk + 1) * MXU_N],
                                 preferred_element_type=F32)

    def head_chunk(c, hd):
        rows = slice(c * CHUNK, (c + 1) * CHUNK)
        ks = slice(hd * HEAD_K, (hd + 1) * HEAD_K)
        vs = slice(hd * HEAD_V, (hd + 1) * HEAD_V)
        scores = jnp.where(rc == 0,
                           _dot_nt(qs_ref[N_LEVELS, rows, ks], ks_ref[N_LEVELS, rows, ks]), 0.0)
        for l in range(N_LEVELS):
            scores = jnp.where((rc >> l) == 1,
                               _dot_nt(qs_ref[l, rows, ks], ks_ref[l, rows, ks]), scores)
        vh = pvg_ref[rows, vs].astype(BF16)
        st = s_ref[hd]
        og_ref[rows, vs] = (jnp.dot(scores.astype(BF16), vh, preferred_element_type=F32)
                            + _dot_nt(qs_ref[N_LEVELS + 1, rows, ks], st.astype(BF16)))
        s_ref[hd] = (eb_ref[c * SUBLANE:c * SUBLANE + 1, ks] * st
                     + _dot_tn(vh, ks_ref[N_LEVELS + 1, rows, ks]))

    def wout_block(blk):
        cs = slice(blk * MXU_N, (blk + 1) * MXU_N)
        o_ref[:, cs] = xprev_ref[:, cs] + jnp.dot(y_ref[...], wo_ref[:, cs],
                                                  preferred_element_type=F32)

    def conv_block(blk):
        cs = slice(blk * MXU_N, (blk + 1) * MXU_N)
        u = pc_ref[:, 2 * D_CONV + blk * MXU_N:2 * D_CONV + (blk + 1) * MXU_N] * pc_ref[:, cs]
        ubuf_ref[SUBLANE:SUBLANE + tm, cs] = u
        cw = cw_ref[:, cs]
        conv = (cw[0:1, :] * ubuf_ref[SUBLANE - 2:SUBLANE - 2 + tm, cs]
                + cw[1:2, :] * ubuf_ref[SUBLANE - 1:SUBLANE - 1 + tm, cs]
                + cw[2:3, :] * u)
        y_ref[:, cs] = (pc_ref[:, D_CONV + blk * MXU_N:D_CONV + (blk + 1) * MXU_N] * conv).astype(BF16)
        ubuf_ref[0:SUBLANE, cs] = ubuf_ref[tm:tm + SUBLANE, cs]

    def prepare_rows(c):
        r0 = c * CHUNK
        rows = slice(r0, r0 + CHUNK)
        la = lap_ref[SUBLANE + r0:SUBLANE + r0 + CHUNK, :]
        la1 = lap_ref[SUBLANE + r0 + 1:SUBLANE + r0 + 1 + CHUNK, :]
        la2 = lap_ref[SUBLANE + r0 + 2:SUBLANE + r0 + 2 + CHUNK, :]
        b = b_ref[rows, :]
        q = pqk_ref[rows, 0:D_QK] * (HEAD_K ** -0.5)
        k = pqk_ref[rows, D_QK:2 * D_QK]
        rowi = lax.broadcasted_iota(jnp.int32, (CHUNK, D_QK), 0)
        r4 = rowi & 3
        for l in range(N_LEVELS):
            if l == 0:
                exponent = jnp.where((rowi & 1) == 0, la1, 0.0)
            elif l == 1:
                exponent = jnp.where(r4 == 0, la1 + la2,
                                     jnp.where(r4 == 1, la1, jnp.where(r4 == 3, la, 0.0)))
            else:
                exponent = -jnp.abs(b - _group_row(b, 2 << l, 1 << l))
            gl = jnp.exp(exponent)
            qs_ref[l, rows, :] = (q * gl).astype(BF16)
            ks_ref[l, rows, :] = (k * gl).astype(BF16)
        qs_ref[N_LEVELS, rows, :] = q.astype(BF16)
        ks_ref[N_LEVELS, rows, :] = k.astype(BF16)
        b_last = jnp.broadcast_to(b[CHUNK - 1:CHUNK, :], (CHUNK, D_QK))
        eb_ref[c * SUBLANE:(c + 1) * SUBLANE, :] = jnp.exp(b_last[0:SUBLANE, :])
        qs_ref[N_LEVELS + 1, rows, :] = (q * jnp.exp(b)).astype(BF16)
        ks_ref[N_LEVELS + 1, rows, :] = (k * jnp.exp(b_last - b)).astype(BF16)

    zl = jnp.dot(h, wz_ref[...], preferred_element_type=F32)
    zz = jnp.dot(zl.astype(BF16), gw2_ref[...], preferred_element_type=F32) + gb_ref[...]
    la = (jnp.minimum(zz, 0.0) - jnp.log(1.0 + jnp.exp(-jnp.abs(zz)))) * (1.0 / GATE_TAU)
    lap_ref[SUBLANE:SUBLANE + tm, :] = la
    la_hi = la.astype(BF16)
    r1 = la - la_hi.astype(F32)
    la_mid = r1.astype(BF16)
    la_lo = (r1 - la_mid.astype(F32)).astype(BF16)
    tri = tri_ref[...]
    b_ref[...] = (jnp.dot(tri, la_hi, preferred_element_type=F32)
                  + jnp.dot(tri, la_mid, preferred_element_type=F32)
                  + jnp.dot(tri, la_lo, preferred_element_type=F32))

    cq_blocks = ([(pqk_ref, OFF_Q, blk) for blk in range(2 * D_QK // MXU_N)]
                 + [(pc_ref, 0, blk) for blk in range(OFF_Q // MXU_N)])
    for c in range(tm // CHUNK):
        for hd in range(N_HEADS):
            project(*cq_blocks[c * N_HEADS + hd])
            head_chunk(c, hd)
        prepare_rows(c)

    for hd in range(N_HEADS):
        vs = slice(hd * HEAD_V, (hd + 1) * HEAD_V)
        gate = pvg_ref[:, D_GLA + hd * HEAD_V:D_GLA + (hd + 1) * HEAD_V]
        y_ref[:, D_CONV + hd * HEAD_V:D_CONV + (hd + 1) * HEAD_V] = (
            _rms(og_ref[:, vs], gn_ref[...]) * (gate * _sigmoid(gate))).astype(BF16)

    for blk in range(D_MODEL // MXU_N):
        wout_block(blk)
    xprev_ref[...] = x_ref[...]
    for blk in range(2 * D_GLA // MXU_N):
        project(pvg_ref, OFF_V, blk)
        if blk < D_CONV // MXU_N:
            conv_block(blk)


def _mix(x, nw, w_in, w_z, conv_w, gw2, gate_b, gla_norm, tri, w_out, *, tm):
    t = x.shape[0]
    n = t // tm

    def resident(shape):
        return pl.BlockSpec(shape, lambda i: (0,) * len(shape), pipeline_mode=pl.Buffered(1))

    return pl.pallas_call(
        functools.partial(_mix_body, tm=tm),
        grid=(n + 1,),
        in_specs=[
            pl.BlockSpec((tm, D_MODEL), lambda i: (jnp.minimum(i, n - 1), 0)),
            resident((1, D_MODEL)),
            resident((D_MODEL, MAIN_COLS)),
            resident((D_MODEL, LANE)),
            resident((CONV_WIDTH, D_CONV)),
            resident((LANE, D_QK)),
            resident((1, D_QK)),
            resident((1, HEAD_V)),
            resident((tm, tm)),
            resident((D_MODEL, D_MODEL)),
        ],
        out_specs=pl.BlockSpec((tm, D_MODEL), lambda i: (jnp.maximum(i - 1, 0), 0)),
        out_shape=jax.ShapeDtypeStruct((t, D_MODEL), F32),
        scratch_shapes=[
            pltpu.VMEM((N_HEADS, HEAD_V, HEAD_K), F32),
            pltpu.VMEM((tm + SUBLANE, D_CONV), F32),
            pltpu.VMEM((tm, 3 * D_CONV), F32),
            pltpu.VMEM((tm, 2 * D_QK), F32),
            pltpu.VMEM((tm, 2 * D_GLA), F32),
            pltpu.VMEM((tm + 2 * SUBLANE, D_QK), F32),
            pltpu.VMEM((tm // CHUNK * SUBLANE, D_QK), F32),
            pltpu.VMEM((N_LEVELS + 2, tm, D_QK), BF16),
            pltpu.VMEM((N_LEVELS + 2, tm, D_QK), BF16),
            pltpu.VMEM((tm, D_GLA), F32),
            pltpu.VMEM((tm, D_MODEL), BF16),
            pltpu.VMEM((tm, D_MODEL), F32),
            pltpu.VMEM((tm, D_QK), F32),
        ],
        compiler_params=pltpu.CompilerParams(
            dimension_semantics=("arbitrary",), vmem_limit_bytes=VMEM_LIMIT),
        name="mix",
    )(x, nw, w_in, w_z, conv_w, gw2, gate_b, gla_norm, tri, w_out)


def kernel(x, ffn1_norm, ffn1_w_gate, ffn1_w_up, ffn1_w_down, mix_norm, w_in, conv_w, gate_w2,
           gate_b, gla_norm, w_out, ffn2_norm, ffn2_w_gate, ffn2_w_up, ffn2_w_down, final_norm):
    b, t, d = x.shape
    depth = w_in.shape[0]
    xs = x.reshape(b * t, d)
    tri = jnp.asarray(_chunk_tril(MIX_TM), BF16)
    fw = final_norm.reshape(1, d)
    ffn1 = (ffn1_w_gate, ffn1_w_up, ffn1_w_down)
    ffn2 = (ffn2_w_gate, ffn2_w_up, ffn2_w_down)
    weights = [w[0].astype(BF16) for w in ffn1]
    w_in_t = jnp.swapaxes(w_in, 1, 2)
    for l in range(depth):
        last = l == depth - 1
        xs, weights, (w_in_bf, w_out_bf, w_tail_t) = _ffn(xs, ffn1_norm[l].reshape(1, d), *weights, fw, final=False,
                                                cast=(ffn2, l), mix_cast=(w_in_t, w_out, l))
        gw2 = jnp.pad(gate_w2[l], ((0, LANE - GATE_RANK), (0, 0))).astype(BF16)
        xs = _mix(xs, mix_norm[l].reshape(1, d), w_in_bf, w_tail_t, conv_w[l], gw2,
                  gate_b[l].reshape(1, D_QK), gla_norm[l].reshape(1, HEAD_V), tri, w_out_bf, tm=MIX_TM)
        xs, weights, _ = _ffn(xs, ffn2_norm[l].reshape(1, d), *weights, fw, final=last,
                              cast=None if last else (ffn1, l + 1))
    return xs.reshape(b, t, d)
```

```python
import functools

import jax
import jax.numpy as jnp
import numpy as np
from jax import lax
from jax.experimental import pallas as pl
from jax.experimental.pallas import tpu as pltpu

D_MODEL = 2048
CHUNK = 64
D_CONV = 1024
D_GLA = 1024
N_HEADS = 4
HEAD_V = 256
HEAD_K = 128
D_QK = 512
GATE_RANK = 16
GATE_TAU = 16.0
CONV_WIDTH = 3
D_FF = 5632
EPS = 1e-6
MAIN_COLS = 3 * D_CONV + 2 * D_QK + 2 * D_GLA
LANE = 128
SUBLANE = 8
N_LEVELS = 6
MXU_N = 256
VMEM_LIMIT = 60 * 1024 * 1024
FFN_TM = 1024
FFN_TF = 512
FFN_FIRST_SLAB = 512
FFN_LAST_SLAB = 512
FFN_FINAL_SLAB = 256
FFN_RES_SCALE = 0.5
FFN_CAST_BLOCKS = 4
MIX_TM = 256

OFF_Q = 3 * D_CONV
OFF_V = OFF_Q + 2 * D_QK

F32 = jnp.float32
BF16 = jnp.bfloat16


def _rms(x, w):
    ms = jnp.mean(x * x, axis=-1, keepdims=True)
    return x * lax.rsqrt(ms + EPS) * w


def _sigmoid(x):
    return 1.0 / (1.0 + jnp.exp(-x))


def _dot_nt(a, b):
    return lax.dot_general(a, b, (((1,), (1,)), ((), ())), preferred_element_type=F32)


def _dot_tn(a, b):
    return lax.dot_general(a, b, (((0,), (0,)), ((), ())), preferred_element_type=F32)


def _ffn_body(x_ref, nw_ref, wg_ref, wu_ref, wd_ref, fw_ref, *refs, n_f, final, n_cast, n_slow_cast):
    n_in = n_cast + n_slow_cast
    cast_in, slow_in = refs[:n_cast], refs[n_cast:n_in]
    o_ref = refs[n_in]
    cast_out, slow_out = refs[n_in + 1:n_in + 1 + n_cast], refs[n_in + 1 + n_cast:2 * n_in + 1]
    h_ref = refs[2 * n_in + 1]
    j = pl.program_id(1)
    tm = x_ref.shape[0]

    for src_ref, dst_ref in zip(cast_in, cast_out, strict=True):
        dst_ref[...] = src_ref[...].astype(BF16)

    def slow_cast():
        if n_slow_cast:
            w_in_t_ref, w_out_ref, tail_t_ref = slow_in
            slow_out[0][...] = w_in_t_ref[...].T.astype(BF16)
            slow_out[1][...] = w_out_ref[...].astype(BF16)
            tail_t = jnp.concatenate(
                [tail_t_ref[...], jnp.zeros((LANE - GATE_RANK, D_MODEL), F32)], axis=0)
            slow_out[2][...] = tail_t.T.astype(BF16)

    def swiglu_down(h):
        g = jnp.dot(h, wg_ref[...], preferred_element_type=F32)
        u = jnp.dot(h, wu_ref[...], preferred_element_type=F32)
        a = (g * _sigmoid(g) * u).astype(BF16)
        return jnp.dot(a, wd_ref[...], preferred_element_type=F32)

    @pl.when(j == 0)
    def _():
        slow_cast()
        for s in range(tm // FFN_FIRST_SLAB):
            rows = slice(s * FFN_FIRST_SLAB, (s + 1) * FFN_FIRST_SLAB)
            h = _rms(x_ref[rows, :], nw_ref[...]).astype(BF16)
            h_ref[rows, :] = h
            o_ref[rows, :] = swiglu_down(h)

    @pl.when(jnp.logical_and(j > 0, j < n_f - 1))
    def _():
        slow_cast()
        o_ref[...] += swiglu_down(h_ref[...])

    @pl.when(j == n_f - 1)
    def _():
        slow_cast()
        last_slab = FFN_FINAL_SLAB if final else FFN_LAST_SLAB
        for s in range(tm // last_slab):
            rows = slice(s * last_slab, (s + 1) * last_slab)
            r = x_ref[rows, :] + FFN_RES_SCALE * (o_ref[rows, :] + swiglu_down(h_ref[rows, :]))
            if final:
                r = _rms(r, fw_ref[...])
            o_ref[rows, :] = r


def _ffn(x, nw, wg, wu, wd, fw, *, final, cast=None, mix_cast=None, tm=FFN_TM, tf=FFN_TF):
    t = x.shape[0]
    n_i, n_f = t // tm, D_FF // tf
    rb = D_MODEL // n_i
    in_specs = [
        pl.BlockSpec((tm, D_MODEL), lambda i, j: (i, 0)),
        pl.BlockSpec((1, D_MODEL), lambda i, j: (0, 0)),
        pl.BlockSpec((D_MODEL, tf), lambda i, j: (0, j)),
        pl.BlockSpec((D_MODEL, tf), lambda i, j: (0, j)),
        pl.BlockSpec((tf, D_MODEL), lambda i, j: (j, 0)),
        pl.BlockSpec((1, D_MODEL), lambda i, j: (0, 0)),
    ]
    out_specs = [pl.BlockSpec((tm, D_MODEL), lambda i, j: (i, 0))]
    out_shape = [jax.ShapeDtypeStruct((t, D_MODEL), F32)]
    cast_args = ()
    if cast is not None:
        cast_args, layer = cast
        assert rb * n_i == D_MODEL and rb % (2 * SUBLANE) == 0 and rb % LANE == 0
        in_specs += [
            pl.BlockSpec((None, rb, tf), lambda i, j: (layer, i, j)),
            pl.BlockSpec((None, rb, tf), lambda i, j: (layer, i, j)),
            pl.BlockSpec((None, tf, rb), lambda i, j: (layer, j, i)),
        ]
        out_specs += [
            pl.BlockSpec((rb, tf), lambda i, j: (i, j)),
            pl.BlockSpec((rb, tf), lambda i, j: (i, j)),
            pl.BlockSpec((tf, rb), lambda i, j: (j, i)),
        ]
        out_shape += [
            jax.ShapeDtypeStruct((D_MODEL, D_FF), BF16),
            jax.ShapeDtypeStruct((D_MODEL, D_FF), BF16),
            jax.ShapeDtypeStruct((D_FF, D_MODEL), BF16),
        ]
    slow_args = ()
    if mix_cast is not None:
        *slow_args, mix_layer = mix_cast
        assert rb * n_i == D_MODEL and rb % (2 * SUBLANE) == 0

        def cblock(j):
            return (j * FFN_CAST_BLOCKS) // n_f

        slow_in, slow_out = [], []
        for cols, transposed in ((MAIN_COLS, True), (D_MODEL, False)):
            cb = cols // FFN_CAST_BLOCKS
            assert cb * FFN_CAST_BLOCKS == cols and cb % LANE == 0 and rb % LANE == 0
            if transposed:
                slow_in.append(pl.BlockSpec((None, cb, rb), lambda i, j: (mix_layer, cblock(j), i)))
            else:
                slow_in.append(pl.BlockSpec((None, rb, cb), lambda i, j: (mix_layer, i, cblock(j))))
            slow_out.append(pl.BlockSpec((rb, cb), lambda i, j: (i, cblock(j))))
            out_shape.append(jax.ShapeDtypeStruct((D_MODEL, cols), BF16))
        slow_args = [slow_args[0], slow_args[1], slow_args[0]]
        slow_in.append(pl.BlockSpec((None, GATE_RANK, D_MODEL),
                                    lambda i, j: (mix_layer, MAIN_COLS // GATE_RANK, 0)))
        slow_out.append(pl.BlockSpec((D_MODEL, LANE), lambda i, j: (0, 0)))
        out_shape.append(jax.ShapeDtypeStruct((D_MODEL, LANE), BF16))
        in_specs += slow_in
        out_specs += slow_out
    n_cast, n_slow = len(cast_args), len(slow_args)
    out = pl.pallas_call(
        functools.partial(_ffn_body, n_f=n_f, final=final, n_cast=n_cast, n_slow_cast=n_slow),
        grid=(n_i, n_f),
        in_specs=in_specs,
        out_specs=out_specs,
        out_shape=out_shape,
        scratch_shapes=[pltpu.VMEM((tm, D_MODEL), BF16)],
        compiler_params=pltpu.CompilerParams(
            dimension_semantics=("arbitrary", "arbitrary"), vmem_limit_bytes=VMEM_LIMIT),
        name="ffn_final" if final else "ffn",
    )(x, nw, wg, wu, wd, fw, *cast_args, *slow_args)
    return out[0], out[1:1 + n_cast], out[1 + n_cast:]


def _chunk_tril(tm):
    t = np.arange(tm)[:, None]
    i = np.arange(tm)[None, :]
    return ((i <= t) & (i // CHUNK == t // CHUNK)).astype(np.float32)


def _group_row(v, group, pick):
    rows, cols = v.shape
    g = v.reshape(rows // group, group, cols)[:, pick:pick + 1, :]
    return jnp.broadcast_to(g, (rows // group, group, cols)).reshape(rows, cols)


def _mix_body(x_ref, nw_ref, win_ref, wz_ref, cw_ref, gw2_ref, gb_ref, gn_ref, tri_ref, wo_ref,
              o_ref, s_ref, ubuf_ref, pc_ref, pqk_ref, pvg_ref, lap_ref, eb_ref, qs_ref, ks_ref,
              og_ref, y_ref, xprev_ref, b_ref, *, tm):
    i = pl.program_id(0)

    @pl.when(i == 0)
    def _():
        for ref in (s_ref, pvg_ref, eb_ref, qs_ref, ks_ref, y_ref, xprev_ref):
            ref[...] = jnp.zeros_like(ref)
        ubuf_ref[0:SUBLANE, :] = jnp.zeros((SUBLANE, D_CONV), F32)
        lap_ref[0:SUBLANE, :] = jnp.zeros((SUBLANE, D_QK), F32)
        lap_ref[SUBLANE + tm:2 * SUBLANE + tm, :] = jnp.zeros((SUBLANE, D_QK), F32)

    h = _rms(x_ref[...], nw_ref[...]).astype(BF16)
    row = lax.broadcasted_iota(jnp.int32, (CHUNK, CHUNK), 0)
    col = lax.broadcasted_iota(jnp.int32, (CHUNK, CHUNK), 1)
    rc = row ^ col

    def project(dst_ref, col0, blk):
        cs = slice(blk * MXU_N, (blk + 1) * MXU_N)
        dst_ref[:, cs] = jnp.dot(h, win_ref[:, col0 + blk * MXU_N:col0 + (blk + 1) * MXU_N],
                                 preferred_element_type=F32)

    def head_chunk(c, hd):
        rows = slice(c * CHUNK, (c + 1) * CHUNK)
        ks = slice(hd * HEAD_K, (hd + 1) * HEAD_K)
        vs = slice(hd * HEAD_V, (hd + 1) * HEAD_V)
        scores = jnp.where(rc == 0,
                           _dot_nt(qs_ref[N_LEVELS, rows, ks], ks_ref[N_LEVELS, rows, ks]), 0.0)
        for l in range(N_LEVELS):
            scores = jnp.where((rc >> l) == 1,
                               _dot_nt(qs_ref[l, rows, ks], ks_ref[l, rows, ks]), scores)
        vh = pvg_ref[rows, vs].astype(BF16)
        st = s_ref[hd]
        og_ref[rows, vs] = (jnp.dot(scores.astype(BF16), vh, preferred_element_type=F32)
                            + _dot_nt(qs_ref[N_LEVELS + 1, rows, ks], st.astype(BF16)))
        s_ref[hd] = (eb_ref[c * SUBLANE:c * SUBLANE + 1, ks] * st
                     + _dot_tn(vh, ks_ref[N_LEVELS + 1, rows, ks]))

    def wout_block(blk):
        cs = slice(blk * MXU_N, (blk + 1) * MXU_N)
        o_ref[:, cs] = xprev_ref[:, cs] + jnp.dot(y_ref[...], wo_ref[:, cs],
                                                  preferred_element_type=F32)

    def conv_block(blk):
        cs = slice(blk * MXU_N, (blk + 1) * MXU_N)
        u = pc_ref[:, 2 * D_CONV + blk * MXU_N:2 * D_CONV + (blk + 1) * MXU_N] * pc_ref[:, cs]
        ubuf_ref[SUBLANE:SUBLANE + tm, cs] = u
        cw = cw_ref[:, cs]
        conv = (cw[0:1, :] * ubuf_ref[SUBLANE - 2:SUBLANE - 2 + tm, cs]
                + cw[1:2, :] * ubuf_ref[SUBLANE - 1:SUBLANE - 1 + tm, cs]
                + cw[2:3, :] * u)
        y_ref[:, cs] = (pc_ref[:, D_CONV + blk * MXU_N:D_CONV + (blk + 1) * MXU_N] * conv).astype(BF16)
        ubuf_ref[0:SUBLANE, cs] = ubuf_ref[tm:tm + SUBLANE, cs]

    def prepare_rows(c):
        r0 = c * CHUNK
        rows = slice(r0, r0 + CHUNK)
        la = lap_ref[SUBLANE + r0:SUBLANE + r0 + CHUNK, :]
        la1 = lap_ref[SUBLANE + r0 + 1:SUBLANE + r0 + 1 + CHUNK, :]
        la2 = lap_ref[SUBLANE + r0 + 2:SUBLANE + r0 + 2 + CHUNK, :]
        b = b_ref[rows, :]
        q = pqk_ref[rows, 0:D_QK] * (HEAD_K ** -0.5)
        k = pqk_ref[rows, D_QK:2 * D_QK]
        rowi = lax.broadcasted_iota(jnp.int32, (CHUNK, D_QK), 0)
        r4 = rowi & 3
        for l in range(N_LEVELS):
            if l == 0:
                exponent = jnp.where((rowi & 1) == 0, la1, 0.0)
            elif l == 1:
                exponent = jnp.where(r4 == 0, la1 + la2,
                                     jnp.where(r4 == 1, la1, jnp.where(r4 == 3, la, 0.0)))
            else:
                exponent = -jnp.abs(b - _group_row(b, 2 << l, 1 << l))
            gl = jnp.exp(exponent)
            qs_ref[l, rows, :] = (q * gl).astype(BF16)
            ks_ref[l, rows, :] = (k * gl).astype(BF16)
        qs_ref[N_LEVELS, rows, :] = q.astype(BF16)
        ks_ref[N_LEVELS, rows, :] = k.astype(BF16)
        b_last = jnp.broadcast_to(b[CHUNK - 1:CHUNK, :], (CHUNK, D_QK))
        eb_ref[c * SUBLANE:(c + 1) * SUBLANE, :] = jnp.exp(b_last[0:SUBLANE, :])
        qs_ref[N_LEVELS + 1, rows, :] = (q * jnp.exp(b)).astype(BF16)
        ks_ref[N_LEVELS + 1, rows, :] = (k * jnp.exp(b_last - b)).astype(BF16)

    zl = jnp.dot(h, wz_ref[...], preferred_element_type=F32)
    zz = jnp.dot(zl.astype(BF16), gw2_ref[...], preferred_element_type=F32) + gb_ref[...]
    la = (jnp.minimum(zz, 0.0) - jnp.log(1.0 + jnp.exp(-jnp.abs(zz)))) * (1.0 / GATE_TAU)
    lap_ref[SUBLANE:SUBLANE + tm, :] = la
    la_hi = la.astype(BF16)
    r1 = la - la_hi.astype(F32)
    la_mid = r1.astype(BF16)
    la_lo = (r1 - la_mid.astype(F32)).astype(BF16)
    tri = tri_ref[...]
    b_ref[...] = (jnp.dot(tri, la_hi, preferred_element_type=F32)
                  + jnp.dot(tri, la_mid, preferred_element_type=F32)
                  + jnp.dot(tri, la_lo, preferred_element_type=F32))

    cq_blocks = ([(pqk_ref, OFF_Q, blk) for blk in range(2 * D_QK // MXU_N)]
                 + [(pc_ref, 0, blk) for blk in range(OFF_Q // MXU_N)])
    for c in range(tm // CHUNK):
        for hd in range(N_HEADS):
            project(*cq_blocks[c * N_HEADS + hd])
            head_chunk(c, hd)
        prepare_rows(c)

    for hd in range(N_HEADS):
        vs = slice(hd * HEAD_V, (hd + 1) * HEAD_V)
        gate = pvg_ref[:, D_GLA + hd * HEAD_V:D_GLA + (hd + 1) * HEAD_V]
        y_ref[:, D_CONV + hd * HEAD_V:D_CONV + (hd + 1) * HEAD_V] = (
            _rms(og_ref[:, vs], gn_ref[...]) * (gate * _sigmoid(gate))).astype(BF16)

    for blk in range(D_MODEL // MXU_N):
        wout_block(blk)
    xprev_ref[...] = x_ref[...]
    for blk in range(2 * D_GLA // MXU_N):
        project(pvg_ref, OFF_V, blk)
        if blk < D_CONV // MXU_N:
            conv_block(blk)


def _mix(x, nw, w_in, w_z, conv_w, gw2, gate_b, gla_norm, tri, w_out, *, tm):
    t = x.shape[0]
    n = t // tm

    def resident(shape):
        return pl.BlockSpec(shape, lambda i: (0,) * len(shape), pipeline_mode=pl.Buffered(1))

    return pl.pallas_call(
        functools.partial(_mix_body, tm=tm),
        grid=(n + 1,),
        in_specs=[
            pl.BlockSpec((tm, D_MODEL), lambda i: (jnp.minimum(i, n - 1), 0)),
            resident((1, D_MODEL)),
            resident((D_MODEL, MAIN_COLS)),
            resident((D_MODEL, LANE)),
            resident((CONV_WIDTH, D_CONV)),
            resident((LANE, D_QK)),
            resident((1, D_QK)),
            resident((1, HEAD_V)),
            resident((tm, tm)),
            resident((D_MODEL, D_MODEL)),
        ],
        out_specs=pl.BlockSpec((tm, D_MODEL), lambda i: (jnp.maximum(i - 1, 0), 0)),
        out_shape=jax.ShapeDtypeStruct((t, D_MODEL), F32),
        scratch_shapes=[
            pltpu.VMEM((N_HEADS, HEAD_V, HEAD_K), F32),
            pltpu.VMEM((tm + SUBLANE, D_CONV), F32),
            pltpu.VMEM((tm, 3 * D_CONV), F32),
            pltpu.VMEM((tm, 2 * D_QK), F32),
            pltpu.VMEM((tm, 2 * D_GLA), F32),
            pltpu.VMEM((tm + 2 * SUBLANE, D_QK), F32),
            pltpu.VMEM((tm // CHUNK * SUBLANE, D_QK), F32),
            pltpu.VMEM((N_LEVELS + 2, tm, D_QK), BF16),
            pltpu.VMEM((N_LEVELS + 2, tm, D_QK), BF16),
            pltpu.VMEM((tm, D_GLA), F32),
            pltpu.VMEM((tm, D_MODEL), BF16),
            pltpu.VMEM((tm, D_MODEL), F32),
            pltpu.VMEM((tm, D_QK), F32),
        ],
        compiler_params=pltpu.CompilerParams(
            dimension_semantics=("arbitrary",), vmem_limit_bytes=VMEM_LIMIT),
        name="mix",
    )(x, nw, w_in, w_z, conv_w, gw2, gate_b, gla_norm, tri, w_out)


def kernel(x, ffn1_norm, ffn1_w_gate, ffn1_w_up, ffn1_w_down, mix_norm, w_in, conv_w, gate_w2,
           gate_b, gla_norm, w_out, ffn2_norm, ffn2_w_gate, ffn2_w_up, ffn2_w_down, final_norm):
    b, t, d = x.shape
    depth = w_in.shape[0]
    xs = x.reshape(b * t, d)
    tri = jnp.asarray(_chunk_tril(MIX_TM), BF16)
    fw = final_norm.reshape(1, d)
    ffn1 = (ffn1_w_gate, ffn1_w_up, ffn1_w_down)
    ffn2 = (ffn2_w_gate, ffn2_w_up, ffn2_w_down)
    weights = [w[0].astype(BF16) for w in ffn1]
    w_in_t = jnp.swapaxes(w_in, 1, 2)
    for l in range(depth):
        last = l == depth - 1
        xs, weights, (w_in_bf, w_out_bf, w_tail_t) = _ffn(xs, ffn1_norm[l].reshape(1, d), *weights, fw, final=False,
                                                cast=(ffn2, l), mix_cast=(w_in_t, w_out, l))
        gw2 = jnp.pad(gate_w2[l], ((0, LANE - GATE_RANK), (0, 0))).astype(BF16)
        xs = _mix(xs, mix_norm[l].reshape(1, d), w_in_bf, w_tail_t, conv_w[l], gw2,
                  gate_b[l].reshape(1, D_QK), gla_norm[l].reshape(1, HEAD_V), tri, w_out_bf, tm=MIX_TM)
        xs, weights, _ = _ffn(xs, ffn2_norm[l].reshape(1, d), *weights, fw, final=last,
                              cast=None if last else (ffn1, l + 1))
    return xs.reshape(b, t, d)
```

```python
import functools

import jax
import jax.numpy as jnp
import numpy as np
from jax import lax
from jax.experimental import pallas as pl
from jax.experimental.pallas import tpu as pltpu

D_MODEL = 2048
CHUNK = 64
D_CONV = 1024
D_GLA = 1024
N_HEADS = 4
HEAD_V = 256
HEAD_K = 128
D_QK = 512
GATE_RANK = 16
GATE_TAU = 16.0
CONV_WIDTH = 3
D_FF = 5632
EPS = 1e-6
MAIN_COLS = 3 * D_CONV + 2 * D_QK + 2 * D_GLA
LANE = 128
SUBLANE = 8
N_LEVELS = 6
MXU_N = 256
VMEM_LIMIT = 60 * 1024 * 1024
FFN_TM = 1024
FFN_TF = 512
FFN_FIRST_SLAB = 512
FFN_LAST_SLAB = 512
FFN_FINAL_SLAB = 256
FFN_RES_SCALE = 0.5
FFN_CAST_BLOCKS = 4
MIX_TM = 256

OFF_Q = 3 * D_CONV
OFF_V = OFF_Q + 2 * D_QK

F32 = jnp.float32
BF16 = jnp.bfloat16


def _rms(x, w):
    ms = jnp.mean(x * x, axis=-1, keepdims=True)
    return x * lax.rsqrt(ms + EPS) * w


def _sigmoid(x):
    return 1.0 / (1.0 + jnp.exp(-x))


def _dot_nt(a, b):
    return lax.dot_general(a, b, (((1,), (1,)), ((), ())), preferred_element_type=F32)


def _dot_tn(a, b):
    return lax.dot_general(a, b, (((0,), (0,)), ((), ())), preferred_element_type=F32)


def _ffn_body(x_ref, nw_ref, wg_ref, wu_ref, wd_ref, fw_ref, *refs, n_f, final, n_cast, n_slow_cast):
    n_in = n_cast + n_slow_cast
    cast_in, slow_in = refs[:n_cast], refs[n_cast:n_in]
    o_ref = refs[n_in]
    cast_out, slow_out = refs[n_in + 1:n_in + 1 + n_cast], refs[n_in + 1 + n_cast:2 * n_in + 1]
    h_ref = refs[2 * n_in + 1]
    j = pl.program_id(1)
    tm = x_ref.shape[0]

    for src_ref, dst_ref in zip(cast_in, cast_out, strict=True):
        dst_ref[...] = src_ref[...].astype(BF16)

    def slow_cast():
        if n_slow_cast:
            w_in_t_ref, w_out_ref, tail_t_ref = slow_in
            slow_out[0][...] = w_in_t_ref[...].T.astype(BF16)
            slow_out[1][...] = w_out_ref[...].astype(BF16)
            tail_t = jnp.concatenate(
                [tail_t_ref[...], jnp.zeros((LANE - GATE_RANK, D_MODEL), F32)], axis=0)
            slow_out[2][...] = tail_t.T.astype(BF16)

    def swiglu_down(h):
        g = jnp.dot(h, wg_ref[...], preferred_element_type=F32)
        u = jnp.dot(h, wu_ref[...], preferred_element_type=F32)
        a = (g * _sigmoid(g) * u).astype(BF16)
        return jnp.dot(a, wd_ref[...], preferred_element_type=F32)

    @pl.when(j == 0)
    def _():
        slow_cast()
        for s in range(tm // FFN_FIRST_SLAB):
            rows = slice(s * FFN_FIRST_SLAB, (s + 1) * FFN_FIRST_SLAB)
            h = _rms(x_ref[rows, :], nw_ref[...]).astype(BF16)
            h_ref[rows, :] = h
            o_ref[rows, :] = swiglu_down(h)

    @pl.when(jnp.logical_and(j > 0, j < n_f - 1))
    def _():
        slow_cast()
        o_ref[...] += swiglu_down(h_ref[...])

    @pl.when(j == n_f - 1)
    def _():
        slow_cast()
        last_slab = FFN_FINAL_SLAB if final else FFN_LAST_SLAB
        for s in range(tm // last_slab):
            rows = slice(s * last_slab, (s + 1) * last_slab)
            r = x_ref[rows, :] + FFN_RES_SCALE * (o_ref[rows, :] + swiglu_down(h_ref[rows, :]))
            if final:
                r = _rms(r, fw_ref[...])
            o_ref[rows, :] = r


def _ffn(x, nw, wg, wu, wd, fw, *, final, cast=None, mix_cast=None, tm=FFN_TM, tf=FFN_TF):
    t = x.shape[0]
    n_i, n_f = t // tm, D_FF // tf
    rb = D_MODEL // n_i
    in_specs = [
        pl.BlockSpec((tm, D_MODEL), lambda i, j: (i, 0)),
        pl.BlockSpec((1, D_MODEL), lambda i, j: (0, 0)),
        pl.BlockSpec((D_MODEL, tf), lambda i, j: (0, j)),
        pl.BlockSpec((D_MODEL, tf), lambda i, j: (0, j)),
        pl.BlockSpec((tf, D_MODEL), lambda i, j: (j, 0)),
        pl.BlockSpec((1, D_MODEL), lambda i, j: (0, 0)),
    ]
    out_specs = [pl.BlockSpec((tm, D_MODEL), lambda i, j: (i, 0))]
    out_shape = [jax.ShapeDtypeStruct((t, D_MODEL), F32)]
    cast_args = ()
    if cast is not None:
        cast_args, layer = cast
        assert rb * n_i == D_MODEL and rb % (2 * SUBLANE) == 0 and rb % LANE == 0
        in_specs += [
            pl.BlockSpec((None, rb, tf), lambda i, j: (layer, i, j)),
            pl.BlockSpec((None, rb, tf), lambda i, j: (layer, i, j)),
            pl.BlockSpec((None, tf, rb), lambda i, j: (layer, j, i)),
        ]
        out_specs += [
            pl.BlockSpec((rb, tf), lambda i, j: (i, j)),
            pl.BlockSpec((rb, tf), lambda i, j: (i, j)),
            pl.BlockSpec((tf, rb), lambda i, j: (j, i)),
        ]
        out_shape += [
            jax.ShapeDtypeStruct((D_MODEL, D_FF), BF16),
            jax.ShapeDtypeStruct((D_MODEL, D_FF), BF16),
            jax.ShapeDtypeStruct((D_FF, D_MODEL), BF16),
        ]
    slow_args = ()
    if mix_cast is not None:
        *slow_args, mix_layer = mix_cast
        assert rb * n_i == D_MODEL and rb % (2 * SUBLANE) == 0

        def cblock(j):
            return (j * FFN_CAST_BLOCKS) // n_f

        slow_in, slow_out = [], []
        for cols, transposed in ((MAIN_COLS, True), (D_MODEL, False)):
            cb = cols // FFN_CAST_BLOCKS
            assert cb * FFN_CAST_BLOCKS == cols and cb % LANE == 0 and rb % LANE == 0
            if transposed:
                slow_in.append(pl.BlockSpec((None, cb, rb), lambda i, j: (mix_layer, cblock(j), i)))
            else:
                slow_in.append(pl.BlockSpec((None, rb, cb), lambda i, j: (mix_layer, i, cblock(j))))
            slow_out.append(pl.BlockSpec((rb, cb), lambda i, j: (i, cblock(j))))
            out_shape.append(jax.ShapeDtypeStruct((D_MODEL, cols), BF16))
        slow_args = [slow_args[0], slow_args[1], slow_args[0]]
        slow_in.append(pl.BlockSpec((None, GATE_RANK, D_MODEL),
                                    lambda i, j: (mix_layer, MAIN_COLS // GATE_RANK, 0)))
        slow_out.append(pl.BlockSpec((D_MODEL, LANE), lambda i, j: (0, 0)))
        out_shape.append(jax.ShapeDtypeStruct((D_MODEL, LANE), BF16))
        in_specs += slow_in
        out_specs += slow_out
    n_cast, n_slow = len(cast_args), len(slow_args)
    out = pl.pallas_call(
        functools.partial(_ffn_body, n_f=n_f, final=final, n_cast=n_cast, n_slow_cast=n_slow),
        grid=(n_i, n_f),
        in_specs=in_specs,
        out_specs=out_specs,
        out_shape=out_shape,
        scratch_shapes=[pltpu.VMEM((tm, D_MODEL), BF16)],
        compiler_params=pltpu.CompilerParams(
            dimension_semantics=("arbitrary", "arbitrary"), vmem_limit_bytes=VMEM_LIMIT),
        name="ffn_final" if final else "ffn",
    )(x, nw, wg, wu, wd, fw, *cast_args, *slow_args)
    return out[0], out[1:1 + n_cast], out[1 + n_cast:]


def _chunk_tril(tm):
    t = np.arange(tm)[:, None]
    i = np.arange(tm)[None, :]
    return ((i <= t) & (i // CHUNK == t // CHUNK)).astype(np.float32)


def _group_row(v, group, pick):
    rows, cols = v.shape
    g = v.reshape(rows // group, group, cols)[:, pick:pick + 1, :]
    return jnp.broadcast_to(g, (rows // group, group, cols)).reshape(rows, cols)


def _mix_body(x_ref, nw_ref, win_ref, wz_ref, cw_ref, gw2_ref, gb_ref, gn_ref, tri_ref, wo_ref,
              o_ref, s_ref, ubuf_ref, pc_ref, pqk_ref, pvg_ref, lap_ref, eb_ref, qs_ref, ks_ref,
              og_ref, y_ref, xprev_ref, b_ref, *, tm):
    i = pl.program_id(0)

    @pl.when(i == 0)
    def _():
        for ref in (s_ref, pvg_ref, eb_ref, qs_ref, ks_ref, y_ref, xprev_ref):
            ref[...] = jnp.zeros_like(ref)
        ubuf_ref[0:SUBLANE, :] = jnp.zeros((SUBLANE, D_CONV), F32)
        lap_ref[0:SUBLANE, :] = jnp.zeros((SUBLANE, D_QK), F32)
        lap_ref[SUBLANE + tm:2 * SUBLANE + tm, :] = jnp.zeros((SUBLANE, D_QK), F32)

    h = _rms(x_ref[...], nw_ref[...]).astype(BF16)
    row = lax.broadcasted_iota(jnp.int32, (CHUNK, CHUNK), 0)
    col = lax.broadcasted_iota(jnp.int32, (CHUNK, CHUNK), 1)
    rc = row ^ col

    def project(dst_ref, col0, blk):
        cs = slice(blk * MXU_N, (blk + 1) * MXU_N)
        dst_ref[:, cs] = jnp.dot(h, win_ref[:, col0 + blk * MXU_N:col0 + (blk + 1) * MXU_N],
                                 preferred_element_type=F32)

    def chunk_scores(c, hd):
        rows = slice(c * CHUNK, (c + 1) * CHUNK)
        ks = slice(hd * HEAD_K, (hd + 1) * HEAD_K)
        scores = jnp.where(rc == 0,
                           _dot_nt(qs_ref[N_LEVELS, rows, ks], ks_ref[N_LEVELS, rows, ks]), 0.0)
        for l in range(N_LEVELS):
            scores = jnp.where((rc >> l) == 1,
                               _dot_nt(qs_ref[l, rows, ks], ks_ref[l, rows, ks]), scores)
        return scores.astype(BF16)

    def head_chunk(c, hd, scores):
        rows = slice(c * CHUNK, (c + 1) * CHUNK)
        ks = slice(hd * HEAD_K, (hd + 1) * HEAD_K)
        vs = slice(hd * HEAD_V, (hd + 1) * HEAD_V)
        vh = pvg_ref[rows, vs].astype(BF16)
        st = s_ref[hd]
        og_ref[rows, vs] = (jnp.dot(scores, vh, preferred_element_type=F32)
                            + _dot_nt(qs_ref[N_LEVELS + 1, rows, ks], st.astype(BF16)))
        s_ref[hd] = (eb_ref[c * SUBLANE:c * SUBLANE + 1, ks] * st
                     + _dot_tn(vh, ks_ref[N_LEVELS + 1, rows, ks]))

    def wout_block(blk):
        cs = slice(blk * MXU_N, (blk + 1) * MXU_N)
        o_ref[:, cs] = xprev_ref[:, cs] + jnp.dot(y_ref[...], wo_ref[:, cs],
                                                  preferred_element_type=F32)

    def conv_block(blk):
        cs = slice(blk * MXU_N, (blk + 1) * MXU_N)
        u = pc_ref[:, 2 * D_CONV + blk * MXU_N:2 * D_CONV + (blk + 1) * MXU_N] * pc_ref[:, cs]
        ubuf_ref[SUBLANE:SUBLANE + tm, cs] = u
        cw = cw_ref[:, cs]
        conv = (cw[0:1, :] * ubuf_ref[SUBLANE - 2:SUBLANE - 2 + tm, cs]
                + cw[1:2, :] * ubuf_ref[SUBLANE - 1:SUBLANE - 1 + tm, cs]
                + cw[2:3, :] * u)
        y_ref[:, cs] = (pc_ref[:, D_CONV + blk * MXU_N:D_CONV + (blk + 1) * MXU_N] * conv).astype(BF16)
        ubuf_ref[0:SUBLANE, cs] = ubuf_ref[tm:tm + SUBLANE, cs]

    def prepare_rows(c):
        r0 = c * CHUNK
        rows = slice(r0, r0 + CHUNK)
        la = lap_ref[SUBLANE + r0:SUBLANE + r0 + CHUNK, :]
        la1 = lap_ref[SUBLANE + r0 + 1:SUBLANE + r0 + 1 + CHUNK, :]
        la2 = lap_ref[SUBLANE + r0 + 2:SUBLANE + r0 + 2 + CHUNK, :]
        b = b_ref[rows, :]
        q = pqk_ref[rows, 0:D_QK] * (HEAD_K ** -0.5)
        k = pqk_ref[rows, D_QK:2 * D_QK]
        rowi = lax.broadcasted_iota(jnp.int32, (CHUNK, D_QK), 0)
        r4 = rowi & 3
        for l in range(N_LEVELS):
            if l == 0:
                exponent = jnp.where((rowi & 1) == 0, la1, 0.0)
            elif l == 1:
                exponent = jnp.where(r4 == 0, la1 + la2,
                                     jnp.where(r4 == 1, la1, jnp.where(r4 == 3, la, 0.0)))
            else:
                exponent = -jnp.abs(b - _group_row(b, 2 << l, 1 << l))
            gl = jnp.exp(exponent)
            qs_ref[l, rows, :] = (q * gl).astype(BF16)
            ks_ref[l, rows, :] = (k * gl).astype(BF16)
        qs_ref[N_LEVELS, rows, :] = q.astype(BF16)
        ks_ref[N_LEVELS, rows, :] = k.astype(BF16)
        b_last = jnp.broadcast_to(b[CHUNK - 1:CHUNK, :], (CHUNK, D_QK))
        eb_ref[c * SUBLANE:(c + 1) * SUBLANE, :] = jnp.exp(b_last[0:SUBLANE, :])
        qs_ref[N_LEVELS + 1, rows, :] = (q * jnp.exp(b)).astype(BF16)
        ks_ref[N_LEVELS + 1, rows, :] = (k * jnp.exp(b_last - b)).astype(BF16)

    zl = jnp.dot(h, wz_ref[...], preferred_element_type=F32)
    zz = jnp.dot(zl.astype(BF16), gw2_ref[...], preferred_element_type=F32) + gb_ref[...]
    la = (jnp.minimum(zz, 0.0) - jnp.log(1.0 + jnp.exp(-jnp.abs(zz)))) * (1.0 / GATE_TAU)
    lap_ref[SUBLANE:SUBLANE + tm, :] = la
    la_hi = la.astype(BF16)
    r1 = la - la_hi.astype(F32)
    la_mid = r1.astype(BF16)
    la_lo = (r1 - la_mid.astype(F32)).astype(BF16)
    tri = tri_ref[...]
    b_ref[...] = (jnp.dot(tri, la_hi, preferred_element_type=F32)
                  + jnp.dot(tri, la_mid, preferred_element_type=F32)
                  + jnp.dot(tri, la_lo, preferred_element_type=F32))

    cq_blocks = ([(pqk_ref, OFF_Q, blk) for blk in range(2 * D_QK // MXU_N)]
                 + [(pc_ref, 0, blk) for blk in range(OFF_Q // MXU_N)])
    head_chunks = [(c, hd) for c in range(tm // CHUNK) for hd in range(N_HEADS)]
    scores = chunk_scores(*head_chunks[0])
    for slot, (c, hd) in enumerate(head_chunks):
        next_scores = chunk_scores(*head_chunks[slot + 1]) if slot + 1 < len(head_chunks) else None
        head_chunk(c, hd, scores)
        project(*cq_blocks[slot])
        scores = next_scores
        if hd == N_HEADS - 1:
            prepare_rows(c)

    for hd in range(N_HEADS):
        vs = slice(hd * HEAD_V, (hd + 1) * HEAD_V)
        gate = pvg_ref[:, D_GLA + hd * HEAD_V:D_GLA + (hd + 1) * HEAD_V]
        y_ref[:, D_CONV + hd * HEAD_V:D_CONV + (hd + 1) * HEAD_V] = (
            _rms(og_ref[:, vs], gn_ref[...]) * (gate * _sigmoid(gate))).astype(BF16)

    for blk in range(D_MODEL // MXU_N):
        wout_block(blk)
    xprev_ref[...] = x_ref[...]
    for blk in range(2 * D_GLA // MXU_N):
        project(pvg_ref, OFF_V, blk)
        if blk < D_CONV // MXU_N:
            conv_block(blk)


def _mix(x, nw, w_in, w_z, conv_w, gw2, gate_b, gla_norm, tri, w_out, *, tm):
    t = x.shape[0]
    n = t // tm

    def resident(shape):
        return pl.BlockSpec(shape, lambda i: (0,) * len(shape), pipeline_mode=pl.Buffered(1))

    return pl.pallas_call(
        functools.partial(_mix_body, tm=tm),
        grid=(n + 1,),
        in_specs=[
            pl.BlockSpec((tm, D_MODEL), lambda i: (jnp.minimum(i, n - 1), 0)),
            resident((1, D_MODEL)),
            resident((D_MODEL, MAIN_COLS)),
            resident((D_MODEL, LANE)),
            resident((CONV_WIDTH, D_CONV)),
            resident((LANE, D_QK)),
            resident((1, D_QK)),
            resident((1, HEAD_V)),
            resident((tm, tm)),
            resident((D_MODEL, D_MODEL)),
        ],
        out_specs=pl.BlockSpec((tm, D_MODEL), lambda i: (jnp.maximum(i - 1, 0), 0)),
        out_shape=jax.ShapeDtypeStruct((t, D_MODEL), F32),
        scratch_shapes=[
            pltpu.VMEM((N_HEADS, HEAD_V, HEAD_K), F32),
            pltpu.VMEM((tm + SUBLANE, D_CONV), F32),
            pltpu.VMEM((tm, 3 * D_CONV), F32),
            pltpu.VMEM((tm, 2 * D_QK), F32),
            pltpu.VMEM((tm, 2 * D_GLA), F32),
            pltpu.VMEM((tm + 2 * SUBLANE, D_QK), F32),
            pltpu.VMEM((tm // CHUNK * SUBLANE, D_QK), F32),
            pltpu.VMEM((N_LEVELS + 2, tm, D_QK), BF16),
            pltpu.VMEM((N_LEVELS + 2, tm, D_QK), BF16),
            pltpu.VMEM((tm, D_GLA), F32),
            pltpu.VMEM((tm, D_MODEL), BF16),
            pltpu.VMEM((tm, D_MODEL), F32),
            pltpu.VMEM((tm, D_QK), F32),
        ],
        compiler_params=pltpu.CompilerParams(
            dimension_semantics=("arbitrary",), vmem_limit_bytes=VMEM_LIMIT),
        name="mix",
    )(x, nw, w_in, w_z, conv_w, gw2, gate_b, gla_norm, tri, w_out)


def kernel(x, ffn1_norm, ffn1_w_gate, ffn1_w_up, ffn1_w_down, mix_norm, w_in, conv_w, gate_w2,
           gate_b, gla_norm, w_out, ffn2_norm, ffn2_w_gate, ffn2_w_up, ffn2_w_down, final_norm):
    b, t, d = x.shape
    depth = w_in.shape[0]
    xs = x.reshape(b * t, d)
    tri = jnp.asarray(_chunk_tril(MIX_TM), BF16)
    fw = final_norm.reshape(1, d)
    ffn1 = (ffn1_w_gate, ffn1_w_up, ffn1_w_down)
    ffn2 = (ffn2_w_gate, ffn2_w_up, ffn2_w_down)
    weights = [w[0].astype(BF16) for w in ffn1]
    w_in_t = jnp.swapaxes(w_in, 1, 2)
    for l in range(depth):
        last = l == depth - 1
        xs, weights, (w_in_bf, w_out_bf, w_tail_t) = _ffn(xs, ffn1_norm[l].reshape(1, d), *weights, fw, final=False,
                                                cast=(ffn2, l), mix_cast=(w_in_t, w_out, l))
        gw2 = jnp.pad(gate_w2[l], ((0, LANE - GATE_RANK), (0, 0))).astype(BF16)
        xs = _mix(xs, mix_norm[l].reshape(1, d), w_in_bf, w_tail_t, conv_w[l], gw2,
                  gate_b[l].reshape(1, D_QK), gla_norm[l].reshape(1, HEAD_V), tri, w_out_bf, tm=MIX_TM)
        xs, weights, _ = _ffn(xs, ffn2_norm[l].reshape(1, d), *weights, fw, final=last,
                              cast=None if last else (ffn1, l + 1))
    return xs.reshape(b, t, d)
```

```python
import functools

import jax
import jax.numpy as jnp
import numpy as np
from jax import lax
from jax.experimental import pallas as pl
from jax.experimental.pallas import tpu as pltpu

D_MODEL = 2048
CHUNK = 64
D_CONV = 1024
D_GLA = 1024
N_HEADS = 4
HEAD_V = 256
HEAD_K = 128
D_QK = 512
GATE_RANK = 16
GATE_TAU = 16.0
CONV_WIDTH = 3
D_FF = 5632
EPS = 1e-6
MAIN_COLS = 3 * D_CONV + 2 * D_QK + 2 * D_GLA
LANE = 128
SUBLANE = 8
N_LEVELS = 6
MXU_N = 256
VMEM_LIMIT = 60 * 1024 * 1024
FFN_TM = 1024
FFN_TF = 512
FFN_FIRST_SLAB = 512
FFN_LAST_SLAB = 512
FFN_FINAL_SLAB = 256
FFN_RES_SCALE = 0.5
FFN_CAST_BLOCKS = 4
MIX_TM = 256

OFF_Q = 3 * D_CONV
OFF_V = OFF_Q + 2 * D_QK

F32 = jnp.float32
BF16 = jnp.bfloat16


def _rms(x, w):
    ms = jnp.mean(x * x, axis=-1, keepdims=True)
    return x * lax.rsqrt(ms + EPS) * w


def _sigmoid(x):
    return 1.0 / (1.0 + jnp.exp(-x))


def _dot_nt(a, b):
    return lax.dot_general(a, b, (((1,), (1,)), ((), ())), preferred_element_type=F32)


def _dot_tn(a, b):
    return lax.dot_general(a, b, (((0,), (0,)), ((), ())), preferred_element_type=F32)


def _ffn_body(x_ref, nw_ref, wg_ref, wu_ref, wd_ref, fw_ref, *refs, n_f, final, n_cast, n_slow_cast):
    n_in = n_cast + n_slow_cast
    cast_in, slow_in = refs[:n_cast], refs[n_cast:n_in]
    o_ref = refs[n_in]
    cast_out, slow_out = refs[n_in + 1:n_in + 1 + n_cast], refs[n_in + 1 + n_cast:2 * n_in + 1]
    h_ref = refs[2 * n_in + 1]
    j = pl.program_id(1)
    tm = x_ref.shape[0]

    for src_ref, dst_ref in zip(cast_in, cast_out, strict=True):
        dst_ref[...] = src_ref[...].astype(BF16)

    def slow_cast():
        if n_slow_cast:
            w_in_t_ref, w_out_ref, tail_t_ref = slow_in
            slow_out[0][...] = w_in_t_ref[...].T.astype(BF16)
            slow_out[1][...] = w_out_ref[...].astype(BF16)
            tail_t = jnp.concatenate(
                [tail_t_ref[...], jnp.zeros((LANE - GATE_RANK, D_MODEL), F32)], axis=0)
            slow_out[2][...] = tail_t.T.astype(BF16)

    def swiglu_down(h):
        g = jnp.dot(h, wg_ref[...], preferred_element_type=F32)
        u = jnp.dot(h, wu_ref[...], preferred_element_type=F32)
        a = (g * _sigmoid(g) * u).astype(BF16)
        return jnp.dot(a, wd_ref[...], preferred_element_type=F32)

    @pl.when(j == 0)
    def _():
        slow_cast()
        for s in range(tm // FFN_FIRST_SLAB):
            rows = slice(s * FFN_FIRST_SLAB, (s + 1) * FFN_FIRST_SLAB)
            h = _rms(x_ref[rows, :], nw_ref[...]).astype(BF16)
            h_ref[rows, :] = h
            o_ref[rows, :] = swiglu_down(h)

    @pl.when(jnp.logical_and(j > 0, j < n_f - 1))
    def _():
        slow_cast()
        o_ref[...] += swiglu_down(h_ref[...])

    @pl.when(j == n_f - 1)
    def _():
        slow_cast()
        last_slab = FFN_FINAL_SLAB if final else FFN_LAST_SLAB
        for s in range(tm // last_slab):
            rows = slice(s * last_slab, (s + 1) * last_slab)
            r = x_ref[rows, :] + FFN_RES_SCALE * (o_ref[rows, :] + swiglu_down(h_ref[rows, :]))
            if final:
                r = _rms(r, fw_ref[...])
            o_ref[rows, :] = r


def _ffn(x, nw, wg, wu, wd, fw, *, final, cast=None, mix_cast=None, tm=FFN_TM, tf=FFN_TF):
    t = x.shape[0]
    n_i, n_f = t // tm, D_FF // tf
    rb = D_MODEL // n_i
    in_specs = [
        pl.BlockSpec((tm, D_MODEL), lambda i, j: (i, 0)),
        pl.BlockSpec((1, D_MODEL), lambda i, j: (0, 0)),
        pl.BlockSpec((D_MODEL, tf), lambda i, j: (0, j)),
        pl.BlockSpec((D_MODEL, tf), lambda i, j: (0, j)),
        pl.BlockSpec((tf, D_MODEL), lambda i, j: (j, 0)),
        pl.BlockSpec((1, D_MODEL), lambda i, j: (0, 0)),
    ]
    out_specs = [pl.BlockSpec((tm, D_MODEL), lambda i, j: (i, 0))]
    out_shape = [jax.ShapeDtypeStruct((t, D_MODEL), F32)]
    cast_args = ()
    if cast is not None:
        cast_args, layer = cast
        assert rb * n_i == D_MODEL and rb % (2 * SUBLANE) == 0 and rb % LANE == 0
        in_specs += [
            pl.BlockSpec((None, rb, tf), lambda i, j: (layer, i, j)),
            pl.BlockSpec((None, rb, tf), lambda i, j: (layer, i, j)),
            pl.BlockSpec((None, tf, rb), lambda i, j: (layer, j, i)),
        ]
        out_specs += [
            pl.BlockSpec((rb, tf), lambda i, j: (i, j)),
            pl.BlockSpec((rb, tf), lambda i, j: (i, j)),
            pl.BlockSpec((tf, rb), lambda i, j: (j, i)),
        ]
        out_shape += [
            jax.ShapeDtypeStruct((D_MODEL, D_FF), BF16),
            jax.ShapeDtypeStruct((D_MODEL, D_FF), BF16),
            jax.ShapeDtypeStruct((D_FF, D_MODEL), BF16),
        ]
    slow_args = ()
    if mix_cast is not None:
        *slow_args, mix_layer = mix_cast
        assert rb * n_i == D_MODEL and rb % (2 * SUBLANE) == 0

        def cblock(j):
            return (j * FFN_CAST_BLOCKS) // n_f

        slow_in, slow_out = [], []
        for cols, transposed in ((MAIN_COLS, True), (D_MODEL, False)):
            cb = cols // FFN_CAST_BLOCKS
            assert cb * FFN_CAST_BLOCKS == cols and cb % LANE == 0 and rb % LANE == 0
            if transposed:
                slow_in.append(pl.BlockSpec((None, cb, rb), lambda i, j: (mix_layer, cblock(j), i)))
            else:
                slow_in.append(pl.BlockSpec((None, rb, cb), lambda i, j: (mix_layer, i, cblock(j))))
            slow_out.append(pl.BlockSpec((rb, cb), lambda i, j: (i, cblock(j))))
            out_shape.append(jax.ShapeDtypeStruct((D_MODEL, cols), BF16))
        slow_args = [slow_args[0], slow_args[1], slow_args[0]]
        slow_in.append(pl.BlockSpec((None, GATE_RANK, D_MODEL),
                                    lambda i, j: (mix_layer, MAIN_COLS // GATE_RANK, 0)))
        slow_out.append(pl.BlockSpec((D_MODEL, LANE), lambda i, j: (0, 0)))
        out_shape.append(jax.ShapeDtypeStruct((D_MODEL, LANE), BF16))
        in_specs += slow_in
        out_specs += slow_out
    n_cast, n_slow = len(cast_args), len(slow_args)
    out = pl.pallas_call(
        functools.partial(_ffn_body, n_f=n_f, final=final, n_cast=n_cast, n_slow_cast=n_slow),
        grid=(n_i, n_f),
        in_specs=in_specs,
        out_specs=out_specs,
        out_shape=out_shape,
        scratch_shapes=[pltpu.VMEM((tm, D_MODEL), BF16)],
        compiler_params=pltpu.CompilerParams(
            dimension_semantics=("arbitrary", "arbitrary"), vmem_limit_bytes=VMEM_LIMIT),
        name="ffn_final" if final else "ffn",
    )(x, nw, wg, wu, wd, fw, *cast_args, *slow_args)
    return out[0], out[1:1 + n_cast], out[1 + n_cast:]


def _chunk_tril(tm):
    t = np.arange(tm)[:, None]
    i = np.arange(tm)[None, :]
    return ((i <= t) & (i // CHUNK == t // CHUNK)).astype(np.float32)


def _group_row(v, group, pick):
    rows, cols = v.shape
    g = v.reshape(rows // group, group, cols)[:, pick:pick + 1, :]
    return jnp.broadcast_to(g, (rows // group, group, cols)).reshape(rows, cols)


def _mix_body(x_ref, nw_ref, win_ref, wz_ref, cw_ref, gw2_ref, gb_ref, gn_ref, tri_ref, wo_ref,
              o_ref, s_ref, ubuf_ref, pc_ref, pqk_ref, pvg_ref, lap_ref, eb_ref, qs_ref, ks_ref,
              og_ref, y_ref, xprev_ref, b_ref, *, tm):
    i = pl.program_id(0)

    @pl.when(i == 0)
    def _():
        for ref in (s_ref, pvg_ref, eb_ref, qs_ref, ks_ref, y_ref, xprev_ref):
            ref[...] = jnp.zeros_like(ref)
        ubuf_ref[0:SUBLANE, :] = jnp.zeros((SUBLANE, D_CONV), F32)
        lap_ref[0:SUBLANE, :] = jnp.zeros((SUBLANE, D_QK), F32)
        lap_ref[SUBLANE + tm:2 * SUBLANE + tm, :] = jnp.zeros((SUBLANE, D_QK), F32)

    h = _rms(x_ref[...], nw_ref[...]).astype(BF16)
    row = lax.broadcasted_iota(jnp.int32, (CHUNK, CHUNK), 0)
    col = lax.broadcasted_iota(jnp.int32, (CHUNK, CHUNK), 1)
    rc = row ^ col

    def project(dst_ref, col0, blk):
        cs = slice(blk * MXU_N, (blk + 1) * MXU_N)
        dst_ref[:, cs] = jnp.dot(h, win_ref[:, col0 + blk * MXU_N:col0 + (blk + 1) * MXU_N],
                                 preferred_element_type=F32)

    def chunk_scores(c, hd):
        rows = slice(c * CHUNK, (c + 1) * CHUNK)
        ks = slice(hd * HEAD_K, (hd + 1) * HEAD_K)
        scores = jnp.where(rc == 0,
                           _dot_nt(qs_ref[N_LEVELS, rows, ks], ks_ref[N_LEVELS, rows, ks]), 0.0)
        for l in range(N_LEVELS):
            scores = jnp.where((rc >> l) == 1,
                               _dot_nt(qs_ref[l, rows, ks], ks_ref[l, rows, ks]), scores)
        return scores.astype(BF16)

    def head_chunk(c, hd, scores):
        rows = slice(c * CHUNK, (c + 1) * CHUNK)
        ks = slice(hd * HEAD_K, (hd + 1) * HEAD_K)
        vs = slice(hd * HEAD_V, (hd + 1) * HEAD_V)
        vh = pvg_ref[rows, vs].astype(BF16)
        st = s_ref[hd]
        og_ref[rows, vs] = (jnp.dot(scores, vh, preferred_element_type=F32)
                            + _dot_nt(qs_ref[N_LEVELS + 1, rows, ks], st.astype(BF16)))
        s_ref[hd] = (eb_ref[c * SUBLANE:c * SUBLANE + 1, ks] * st
                     + _dot_tn(vh, ks_ref[N_LEVELS + 1, rows, ks]))

    def wout_block(blk):
        cs = slice(blk * MXU_N, (blk + 1) * MXU_N)
        o_ref[:, cs] = xprev_ref[:, cs] + jnp.dot(y_ref[...], wo_ref[:, cs],
                                                  preferred_element_type=F32)

    def conv_block(blk):
        cs = slice(blk * MXU_N, (blk + 1) * MXU_N)
        u = pc_ref[:, 2 * D_CONV + blk * MXU_N:2 * D_CONV + (blk + 1) * MXU_N] * pc_ref[:, cs]
        ubuf_ref[SUBLANE:SUBLANE + tm, cs] = u
        cw = cw_ref[:, cs]
        conv = (cw[0:1, :] * ubuf_ref[SUBLANE - 2:SUBLANE - 2 + tm, cs]
                + cw[1:2, :] * ubuf_ref[SUBLANE - 1:SUBLANE - 1 + tm, cs]
                + cw[2:3, :] * u)
        y_ref[:, cs] = (pc_ref[:, D_CONV + blk * MXU_N:D_CONV + (blk + 1) * MXU_N] * conv).astype(BF16)
        ubuf_ref[0:SUBLANE, cs] = ubuf_ref[tm:tm + SUBLANE, cs]

    def prepare_rows(c):
        r0 = c * CHUNK
        rows = slice(r0, r0 + CHUNK)
        la = lap_ref[SUBLANE + r0:SUBLANE + r0 + CHUNK, :]
        la1 = lap_ref[SUBLANE + r0 + 1:SUBLANE + r0 + 1 + CHUNK, :]
        la2 = lap_ref[SUBLANE + r0 + 2:SUBLANE + r0 + 2 + CHUNK, :]
        b = b_ref[rows, :]
        q = pqk_ref[rows, 0:D_QK] * (HEAD_K ** -0.5)
        k = pqk_ref[rows, D_QK:2 * D_QK]
        rowi = lax.broadcasted_iota(jnp.int32, (CHUNK, D_QK), 0)
        r4 = rowi & 3
        for l in range(N_LEVELS):
            if l == 0:
                exponent = jnp.where((rowi & 1) == 0, la1, 0.0)
            elif l == 1:
                exponent = jnp.where(r4 == 0, la1 + la2,
                                     jnp.where(r4 == 1, la1, jnp.where(r4 == 3, la, 0.0)))
            else:
                exponent = -jnp.abs(b - _group_row(b, 2 << l, 1 << l))
            gl = jnp.exp(exponent)
            qs_ref[l, rows, :] = (q * gl).astype(BF16)
            ks_ref[l, rows, :] = (k * gl).astype(BF16)
        qs_ref[N_LEVELS, rows, :] = q.astype(BF16)
        ks_ref[N_LEVELS, rows, :] = k.astype(BF16)
        b_last = jnp.broadcast_to(b[CHUNK - 1:CHUNK, :], (CHUNK, D_QK))
        eb_ref[c * SUBLANE:(c + 1) * SUBLANE, :] = jnp.exp(b_last[0:SUBLANE, :])
        qs_ref[N_LEVELS + 1, rows, :] = (q * jnp.exp(b)).astype(BF16)
        ks_ref[N_LEVELS + 1, rows, :] = (k * jnp.exp(b_last - b)).astype(BF16)

    zl = jnp.dot(h, wz_ref[...], preferred_element_type=F32)
    zz = jnp.dot(zl.astype(BF16), gw2_ref[...], preferred_element_type=F32) + gb_ref[...]
    la = (jnp.minimum(zz, 0.0) - jnp.log(1.0 + jnp.exp(-jnp.abs(zz)))) * (1.0 / GATE_TAU)
    lap_ref[SUBLANE:SUBLANE + tm, :] = la
    la_hi = la.astype(BF16)
    r1 = la - la_hi.astype(F32)
    la_mid = r1.astype(BF16)
    la_lo = (r1 - la_mid.astype(F32)).astype(BF16)
    tri = tri_ref[...]
    b_ref[...] = (jnp.dot(tri, la_hi, preferred_element_type=F32)
                  + jnp.dot(tri, la_mid, preferred_element_type=F32)
                  + jnp.dot(tri, la_lo, preferred_element_type=F32))

    cq_blocks = ([(pqk_ref, OFF_Q, blk) for blk in range(2 * D_QK // MXU_N)]
                 + [(pc_ref, 0, blk) for blk in range(OFF_Q // MXU_N)])
    head_chunks = [(c, hd) for c in range(tm // CHUNK) for hd in range(N_HEADS)]
    scores = chunk_scores(*head_chunks[0])
    for slot, (c, hd) in enumerate(head_chunks):
        next_scores = chunk_scores(*head_chunks[slot + 1]) if slot + 1 < len(head_chunks) else None
        project(*cq_blocks[slot])
        head_chunk(c, hd, scores)
        scores = next_scores
        if hd == N_HEADS - 1:
            prepare_rows(c)

    for hd in range(N_HEADS):
        vs = slice(hd * HEAD_V, (hd + 1) * HEAD_V)
        gate = pvg_ref[:, D_GLA + hd * HEAD_V:D_GLA + (hd + 1) * HEAD_V]
        y_ref[:, D_CONV + hd * HEAD_V:D_CONV + (hd + 1) * HEAD_V] = (
            _rms(og_ref[:, vs], gn_ref[...]) * (gate * _sigmoid(gate))).astype(BF16)

    for blk in range(D_MODEL // MXU_N):
        wout_block(blk)
    xprev_ref[...] = x_ref[...]
    for blk in range(2 * D_GLA // MXU_N):
        project(pvg_ref, OFF_V, blk)
        if blk < D_CONV // MXU_N:
            conv_block(blk)


def _mix(x, nw, w_in, w_z, conv_w, gw2, gate_b, gla_norm, tri, w_out, *, tm):
    t = x.shape[0]
    n = t // tm

    def resident(shape):
        return pl.BlockSpec(shape, lambda i: (0,) * len(shape), pipeline_mode=pl.Buffered(1))

    return pl.pallas_call(
        functools.partial(_mix_body, tm=tm),
        grid=(n + 1,),
        in_specs=[
            pl.BlockSpec((tm, D_MODEL), lambda i: (jnp.minimum(i, n - 1), 0)),
            resident((1, D_MODEL)),
            resident((D_MODEL, MAIN_COLS)),
            resident((D_MODEL, LANE)),
            resident((CONV_WIDTH, D_CONV)),
            resident((LANE, D_QK)),
            resident((1, D_QK)),
            resident((1, HEAD_V)),
            resident((tm, tm)),
            resident((D_MODEL, D_MODEL)),
        ],
        out_specs=pl.BlockSpec((tm, D_MODEL), lambda i: (jnp.maximum(i - 1, 0), 0)),
        out_shape=jax.ShapeDtypeStruct((t, D_MODEL), F32),
        scratch_shapes=[
            pltpu.VMEM((N_HEADS, HEAD_V, HEAD_K), F32),
            pltpu.VMEM((tm + SUBLANE, D_CONV), F32),
            pltpu.VMEM((tm, 3 * D_CONV), F32),
            pltpu.VMEM((tm, 2 * D_QK), F32),
            pltpu.VMEM((tm, 2 * D_GLA), F32),
            pltpu.VMEM((tm + 2 * SUBLANE, D_QK), F32),
            pltpu.VMEM((tm // CHUNK * SUBLANE, D_QK), F32),
            pltpu.VMEM((N_LEVELS + 2, tm, D_QK), BF16),
            pltpu.VMEM((N_LEVELS + 2, tm, D_QK), BF16),
            pltpu.VMEM((tm, D_GLA), F32),
            pltpu.VMEM((tm, D_MODEL), BF16),
            pltpu.VMEM((tm, D_MODEL), F32),
            pltpu.VMEM((tm, D_QK), F32),
        ],
        compiler_params=pltpu.CompilerParams(
            dimension_semantics=("arbitrary",), vmem_limit_bytes=VMEM_LIMIT),
        name="mix",
    )(x, nw, w_in, w_z, conv_w, gw2, gate_b, gla_norm, tri, w_out)


def kernel(x, ffn1_norm, ffn1_w_gate, ffn1_w_up, ffn1_w_down, mix_norm, w_in, conv_w, gate_w2,
           gate_b, gla_norm, w_out, ffn2_norm, ffn2_w_gate, ffn2_w_up, ffn2_w_down, final_norm):
    b, t, d = x.shape
    depth = w_in.shape[0]
    xs = x.reshape(b * t, d)
    tri = jnp.asarray(_chunk_tril(MIX_TM), BF16)
    fw = final_norm.reshape(1, d)
    ffn1 = (ffn1_w_gate, ffn1_w_up, ffn1_w_down)
    ffn2 = (ffn2_w_gate, ffn2_w_up, ffn2_w_down)
    weights = [w[0].astype(BF16) for w in ffn1]
    w_in_t = jnp.swapaxes(w_in, 1, 2)
    for l in range(depth):
        last = l == depth - 1
        xs, weights, (w_in_bf, w_out_bf, w_tail_t) = _ffn(xs, ffn1_norm[l].reshape(1, d), *weights, fw, final=False,
                                                cast=(ffn2, l), mix_cast=(w_in_t, w_out, l))
        gw2 = jnp.pad(gate_w2[l], ((0, LANE - GATE_RANK), (0, 0))).astype(BF16)
        xs = _mix(xs, mix_norm[l].reshape(1, d), w_in_bf, w_tail_t, conv_w[l], gw2,
                  gate_b[l].reshape(1, D_QK), gla_norm[l].reshape(1, HEAD_V), tri, w_out_bf, tm=MIX_TM)
        xs, weights, _ = _ffn(xs, ffn2_norm[l].reshape(1, d), *weights, fw, final=last,
                              cast=None if last else (ffn1, l + 1))
    return xs.reshape(b, t, d)
```

```python
import functools

import jax
import jax.numpy as jnp
import numpy as np
from jax import lax
from jax.experimental import pallas as pl
from jax.experimental.pallas import tpu as pltpu

D_MODEL = 2048
CHUNK = 64
D_CONV = 1024
D_GLA = 1024
N_HEADS = 4
HEAD_V = 256
HEAD_K = 128
D_QK = 512
GATE_RANK = 16
GATE_TAU = 16.0
CONV_WIDTH = 3
D_FF = 5632
EPS = 1e-6
MAIN_COLS = 3 * D_CONV + 2 * D_QK + 2 * D_GLA
LANE = 128
SUBLANE = 8
N_LEVELS = 6
MXU_N = 256
VMEM_LIMIT = 60 * 1024 * 1024
FFN_TM = 1024
FFN_TF = 512
FFN_FIRST_SLAB = 512
FFN_LAST_SLAB = 512
FFN_FINAL_SLAB = 256
FFN_RES_SCALE = 0.5
FFN_CAST_BLOCKS = 4
MIX_TM = 256

OFF_Q = 3 * D_CONV
OFF_V = OFF_Q + 2 * D_QK

F32 = jnp.float32
BF16 = jnp.bfloat16


def _rms(x, w):
    ms = jnp.mean(x * x, axis=-1, keepdims=True)
    return x * lax.rsqrt(ms + EPS) * w


def _sigmoid(x):
    return 1.0 / (1.0 + jnp.exp(-x))


def _dot_nt(a, b):
    return lax.dot_general(a, b, (((1,), (1,)), ((), ())), preferred_element_type=F32)


def _dot_tn(a, b):
    return lax.dot_general(a, b, (((0,), (0,)), ((), ())), preferred_element_type=F32)


def _ffn_body(x_ref, nw_ref, wg_ref, wu_ref, wd_ref, fw_ref, *refs, n_f, final, n_cast, n_slow_cast):
    n_in = n_cast + n_slow_cast
    cast_in, slow_in = refs[:n_cast], refs[n_cast:n_in]
    o_ref = refs[n_in]
    cast_out, slow_out = refs[n_in + 1:n_in + 1 + n_cast], refs[n_in + 1 + n_cast:2 * n_in + 1]
    h_ref = refs[2 * n_in + 1]
    j = pl.program_id(1)
    tm = x_ref.shape[0]

    for src_ref, dst_ref in zip(cast_in, cast_out, strict=True):
        dst_ref[...] = src_ref[...].astype(BF16)

    def slow_cast():
        if n_slow_cast:
            w_in_t_ref, w_out_ref, tail_t_ref = slow_in
            slow_out[0][...] = w_in_t_ref[...].T.astype(BF16)
            slow_out[1][...] = w_out_ref[...].astype(BF16)
            tail_t = jnp.concatenate(
                [tail_t_ref[...], jnp.zeros((LANE - GATE_RANK, D_MODEL), F32)], axis=0)
            slow_out[2][...] = tail_t.T.astype(BF16)

    def swiglu_down(h):
        g = jnp.dot(h, wg_ref[...], preferred_element_type=F32)
        u = jnp.dot(h, wu_ref[...], preferred_element_type=F32)
        a = (g * _sigmoid(g) * u).astype(BF16)
        return jnp.dot(a, wd_ref[...], preferred_element_type=F32)

    @pl.when(j == 0)
    def _():
        slow_cast()
        for s in range(tm // FFN_FIRST_SLAB):
            rows = slice(s * FFN_FIRST_SLAB, (s + 1) * FFN_FIRST_SLAB)
            h = _rms(x_ref[rows, :], nw_ref[...]).astype(BF16)
            h_ref[rows, :] = h
            o_ref[rows, :] = swiglu_down(h)

    @pl.when(jnp.logical_and(j > 0, j < n_f - 1))
    def _():
        slow_cast()
        o_ref[...] += swiglu_down(h_ref[...])

    @pl.when(j == n_f - 1)
    def _():
        slow_cast()
        last_slab = FFN_FINAL_SLAB if final else FFN_LAST_SLAB
        for s in range(tm // last_slab):
            rows = slice(s * last_slab, (s + 1) * last_slab)
            r = x_ref[rows, :] + FFN_RES_SCALE * (o_ref[rows, :] + swiglu_down(h_ref[rows, :]))
            if final:
                r = _rms(r, fw_ref[...])
            o_ref[rows, :] = r


def _ffn(x, nw, wg, wu, wd, fw, *, final, cast=None, mix_cast=None, tm=FFN_TM, tf=FFN_TF):
    t = x.shape[0]
    n_i, n_f = t // tm, D_FF // tf
    rb = D_MODEL // n_i
    in_specs = [
        pl.BlockSpec((tm, D_MODEL), lambda i, j: (i, 0)),
        pl.BlockSpec((1, D_MODEL), lambda i, j: (0, 0)),
        pl.BlockSpec((D_MODEL, tf), lambda i, j: (0, j)),
        pl.BlockSpec((D_MODEL, tf), lambda i, j: (0, j)),
        pl.BlockSpec((tf, D_MODEL), lambda i, j: (j, 0)),
        pl.BlockSpec((1, D_MODEL), lambda i, j: (0, 0)),
    ]
    out_specs = [pl.BlockSpec((tm, D_MODEL), lambda i, j: (i, 0))]
    out_shape = [jax.ShapeDtypeStruct((t, D_MODEL), F32)]
    cast_args = ()
    if cast is not None:
        cast_args, layer = cast
        assert rb * n_i == D_MODEL and rb % (2 * SUBLANE) == 0 and rb % LANE == 0
        in_specs += [
            pl.BlockSpec((None, rb, tf), lambda i, j: (layer, i, j)),
            pl.BlockSpec((None, rb, tf), lambda i, j: (layer, i, j)),
            pl.BlockSpec((None, tf, rb), lambda i, j: (layer, j, i)),
        ]
        out_specs += [
            pl.BlockSpec((rb, tf), lambda i, j: (i, j)),
            pl.BlockSpec((rb, tf), lambda i, j: (i, j)),
            pl.BlockSpec((tf, rb), lambda i, j: (j, i)),
        ]
        out_shape += [
            jax.ShapeDtypeStruct((D_MODEL, D_FF), BF16),
            jax.ShapeDtypeStruct((D_MODEL, D_FF), BF16),
            jax.ShapeDtypeStruct((D_FF, D_MODEL), BF16),
        ]
    slow_args = ()
    if mix_cast is not None:
        *slow_args, mix_layer = mix_cast
        assert rb * n_i == D_MODEL and rb % (2 * SUBLANE) == 0

        def cblock(j):
            return (j * FFN_CAST_BLOCKS) // n_f

        slow_in, slow_out = [], []
        for cols, transposed in ((MAIN_COLS, True), (D_MODEL, False)):
            cb = cols // FFN_CAST_BLOCKS
            assert cb * FFN_CAST_BLOCKS == cols and cb % LANE == 0 and rb % LANE == 0
            if transposed:
                slow_in.append(pl.BlockSpec((None, cb, rb), lambda i, j: (mix_layer, cblock(j), i)))
            else:
                slow_in.append(pl.BlockSpec((None, rb, cb), lambda i, j: (mix_layer, i, cblock(j))))
            slow_out.append(pl.BlockSpec((rb, cb), lambda i, j: (i, cblock(j))))
            out_shape.append(jax.ShapeDtypeStruct((D_MODEL, cols), BF16))
        slow_args = [slow_args[0], slow_args[1], slow_args[0]]
        slow_in.append(pl.BlockSpec((None, GATE_RANK, D_MODEL),
                                    lambda i, j: (mix_layer, MAIN_COLS // GATE_RANK, 0)))
        slow_out.append(pl.BlockSpec((D_MODEL, LANE), lambda i, j: (0, 0)))
        out_shape.append(jax.ShapeDtypeStruct((D_MODEL, LANE), BF16))
        in_specs += slow_in
        out_specs += slow_out
    n_cast, n_slow = len(cast_args), len(slow_args)
    out = pl.pallas_call(
        functools.partial(_ffn_body, n_f=n_f, final=final, n_cast=n_cast, n_slow_cast=n_slow),
        grid=(n_i, n_f),
        in_specs=in_specs,
        out_specs=out_specs,
        out_shape=out_shape,
        scratch_shapes=[pltpu.VMEM((tm, D_MODEL), BF16)],
        compiler_params=pltpu.CompilerParams(
            dimension_semantics=("arbitrary", "arbitrary"), vmem_limit_bytes=VMEM_LIMIT),
        name="ffn_final" if final else "ffn",
    )(x, nw, wg, wu, wd, fw, *cast_args, *slow_args)
    return out[0], out[1:1 + n_cast], out[1 + n_cast:]


def _chunk_tril(tm):
    t = np.arange(tm)[:, None]
    i = np.arange(tm)[None, :]
    return ((i <= t) & (i // CHUNK == t // CHUNK)).astype(np.float32)


def _group_row(v, group, pick):
    rows, cols = v.shape
    g = v.reshape(rows // group, group, cols)[:, pick:pick + 1, :]
    return jnp.broadcast_to(g, (rows // group, group, cols)).reshape(rows, cols)


def _mix_body(x_ref, nw_ref, win_ref, wz_ref, cw_ref, gw2_ref, gb_ref, gn_ref, tri_ref, wo_ref,
              o_ref, s_ref, ubuf_ref, pc_ref, pqk_ref, pvg_ref, lap_ref, eb_ref, qs_ref, ks_ref,
              og_ref, y_ref, xprev_ref, b_ref, *, tm):
    i = pl.program_id(0)

    @pl.when(i == 0)
    def _():
        for ref in (s_ref, pvg_ref, eb_ref, qs_ref, ks_ref, y_ref, xprev_ref):
            ref[...] = jnp.zeros_like(ref)
        ubuf_ref[0:SUBLANE, :] = jnp.zeros((SUBLANE, D_CONV), F32)
        lap_ref[0:SUBLANE, :] = jnp.zeros((SUBLANE, D_QK), F32)
        lap_ref[SUBLANE + tm:2 * SUBLANE + tm, :] = jnp.zeros((SUBLANE, D_QK), F32)

    h = _rms(x_ref[...], nw_ref[...]).astype(BF16)
    row = lax.broadcasted_iota(jnp.int32, (CHUNK, CHUNK), 0)
    col = lax.broadcasted_iota(jnp.int32, (CHUNK, CHUNK), 1)
    rc = row ^ col

    def project(dst_ref, col0, blk):
        cs = slice(blk * MXU_N, (blk + 1) * MXU_N)
        dst_ref[:, cs] = jnp.dot(h, win_ref[:, col0 + blk * MXU_N:col0 + (blk + 1) * MXU_N],
                                 preferred_element_type=F32)

    def chunk_scores(c, hd):
        rows = slice(c * CHUNK, (c + 1) * CHUNK)
        ks = slice(hd * HEAD_K, (hd + 1) * HEAD_K)
        scores = jnp.where(rc == 0,
                           _dot_nt(qs_ref[N_LEVELS, rows, ks], ks_ref[N_LEVELS, rows, ks]), 0.0)
        for l in range(N_LEVELS):
            scores = jnp.where((rc >> l) == 1,
                               _dot_nt(qs_ref[l, rows, ks], ks_ref[l, rows, ks]), scores)
        return scores.astype(BF16)

    def head_chunk(c, hd, scores):
        rows = slice(c * CHUNK, (c + 1) * CHUNK)
        ks = slice(hd * HEAD_K, (hd + 1) * HEAD_K)
        vs = slice(hd * HEAD_V, (hd + 1) * HEAD_V)
        vh = pvg_ref[rows, vs].astype(BF16)
        st = s_ref[hd]
        og_ref[rows, vs] = (jnp.dot(scores, vh, preferred_element_type=F32)
                            + _dot_nt(qs_ref[N_LEVELS + 1, rows, ks], st.astype(BF16)))
        s_ref[hd] = (eb_ref[c * SUBLANE:c * SUBLANE + 1, ks] * st
                     + _dot_tn(vh, ks_ref[N_LEVELS + 1, rows, ks]))

    def wout_block(blk):
        cs = slice(blk * MXU_N, (blk + 1) * MXU_N)
        o_ref[:, cs] = xprev_ref[:, cs] + jnp.dot(y_ref[...], wo_ref[:, cs],
                                                  preferred_element_type=F32)

    def conv_block(blk):
        cs = slice(blk * MXU_N, (blk + 1) * MXU_N)
        u = pc_ref[:, 2 * D_CONV + blk * MXU_N:2 * D_CONV + (blk + 1) * MXU_N] * pc_ref[:, cs]
        ubuf_ref[SUBLANE:SUBLANE + tm, cs] = u
        cw = cw_ref[:, cs]
        conv = (cw[0:1, :] * ubuf_ref[SUBLANE - 2:SUBLANE - 2 + tm, cs]
                + cw[1:2, :] * ubuf_ref[SUBLANE - 1:SUBLANE - 1 + tm, cs]
                + cw[2:3, :] * u)
        y_ref[:, cs] = (pc_ref[:, D_CONV + blk * MXU_N:D_CONV + (blk + 1) * MXU_N] * conv).astype(BF16)
        ubuf_ref[0:SUBLANE, cs] = ubuf_ref[tm:tm + SUBLANE, cs]

    def prepare_rows(c):
        r0 = c * CHUNK
        rows = slice(r0, r0 + CHUNK)
        la = lap_ref[SUBLANE + r0:SUBLANE + r0 + CHUNK, :]
        la1 = lap_ref[SUBLANE + r0 + 1:SUBLANE + r0 + 1 + CHUNK, :]
        la2 = lap_ref[SUBLANE + r0 + 2:SUBLANE + r0 + 2 + CHUNK, :]
        b = b_ref[rows, :]
        q = pqk_ref[rows, 0:D_QK] * (HEAD_K ** -0.5)
        k = pqk_ref[rows, D_QK:2 * D_QK]
        rowi = lax.broadcasted_iota(jnp.int32, (CHUNK, D_QK), 0)
        r4 = rowi & 3
        for l in range(N_LEVELS):
            if l == 0:
                exponent = jnp.where((rowi & 1) == 0, la1, 0.0)
            elif l == 1:
                exponent = jnp.where(r4 == 0, la1 + la2,
                                     jnp.where(r4 == 1, la1, jnp.where(r4 == 3, la, 0.0)))
            else:
                exponent = -jnp.abs(b - _group_row(b, 2 << l, 1 << l))
            gl = jnp.exp(exponent)
            qs_ref[l, rows, :] = (q * gl).astype(BF16)
            ks_ref[l, rows, :] = (k * gl).astype(BF16)
        qs_ref[N_LEVELS, rows, :] = q.astype(BF16)
        ks_ref[N_LEVELS, rows, :] = k.astype(BF16)
        b_last = jnp.broadcast_to(b[CHUNK - 1:CHUNK, :], (CHUNK, D_QK))
        eb_ref[c * SUBLANE:(c + 1) * SUBLANE, :] = jnp.exp(b_last[0:SUBLANE, :])
        qs_ref[N_LEVELS + 1, rows, :] = (q * jnp.exp(b)).astype(BF16)
        ks_ref[N_LEVELS + 1, rows, :] = (k * jnp.exp(b_last - b)).astype(BF16)

    zl = jnp.dot(h, wz_ref[...], preferred_element_type=F32)
    zz = jnp.dot(zl.astype(BF16), gw2_ref[...], preferred_element_type=F32) + gb_ref[...]
    la = (jnp.minimum(zz, 0.0) - jnp.log(1.0 + jnp.exp(-jnp.abs(zz)))) * (1.0 / GATE_TAU)
    lap_ref[SUBLANE:SUBLANE + tm, :] = la
    la_hi = la.astype(BF16)
    r1 = la - la_hi.astype(F32)
    la_mid = r1.astype(BF16)
    la_lo = (r1 - la_mid.astype(F32)).astype(BF16)
    tri = tri_ref[...]
    b_ref[...] = (jnp.dot(tri, la_hi, preferred_element_type=F32)
                  + jnp.dot(tri, la_mid, preferred_element_type=F32)
                  + jnp.dot(tri, la_lo, preferred_element_type=F32))

    cq_blocks = ([(pqk_ref, OFF_Q, blk) for blk in range(2 * D_QK // MXU_N)]
                 + [(pc_ref, 0, blk) for blk in range(OFF_Q // MXU_N)])
    head_chunks = [(c, hd) for c in range(tm // CHUNK) for hd in range(N_HEADS)]
    scores = chunk_scores(*head_chunks[0])
    for slot, (c, hd) in enumerate(head_chunks):
        next_scores = chunk_scores(*head_chunks[slot + 1]) if slot + 1 < len(head_chunks) else None
        project(*cq_blocks[slot])
        head_chunk(c, hd, scores)
        scores = next_scores
        if hd == N_HEADS - 1:
            prepare_rows(c)

    n_v_blocks = D_GLA // MXU_N
    for blk in range(n_v_blocks):
        project(pvg_ref, OFF_V, blk)

    for hd in range(N_HEADS):
        vs = slice(hd * HEAD_V, (hd + 1) * HEAD_V)
        gate = pvg_ref[:, D_GLA + hd * HEAD_V:D_GLA + (hd + 1) * HEAD_V]
        y_ref[:, D_CONV + hd * HEAD_V:D_CONV + (hd + 1) * HEAD_V] = (
            _rms(og_ref[:, vs], gn_ref[...]) * (gate * _sigmoid(gate))).astype(BF16)

    for blk in range(D_MODEL // MXU_N):
        wout_block(blk)
    xprev_ref[...] = x_ref[...]
    for blk in range(n_v_blocks, 2 * D_GLA // MXU_N):
        project(pvg_ref, OFF_V, blk)
        conv_block(blk - n_v_blocks)


def _mix(x, nw, w_in, w_z, conv_w, gw2, gate_b, gla_norm, tri, w_out, *, tm):
    t = x.shape[0]
    n = t // tm

    def resident(shape):
        return pl.BlockSpec(shape, lambda i: (0,) * len(shape), pipeline_mode=pl.Buffered(1))

    return pl.pallas_call(
        functools.partial(_mix_body, tm=tm),
        grid=(n + 1,),
        in_specs=[
            pl.BlockSpec((tm, D_MODEL), lambda i: (jnp.minimum(i, n - 1), 0)),
            resident((1, D_MODEL)),
            resident((D_MODEL, MAIN_COLS)),
            resident((D_MODEL, LANE)),
            resident((CONV_WIDTH, D_CONV)),
            resident((LANE, D_QK)),
            resident((1, D_QK)),
            resident((1, HEAD_V)),
            resident((tm, tm)),
            resident((D_MODEL, D_MODEL)),
        ],
        out_specs=pl.BlockSpec((tm, D_MODEL), lambda i: (jnp.maximum(i - 1, 0), 0)),
        out_shape=jax.ShapeDtypeStruct((t, D_MODEL), F32),
        scratch_shapes=[
            pltpu.VMEM((N_HEADS, HEAD_V, HEAD_K), F32),
            pltpu.VMEM((tm + SUBLANE, D_CONV), F32),
            pltpu.VMEM((tm, 3 * D_CONV), F32),
            pltpu.VMEM((tm, 2 * D_QK), F32),
            pltpu.VMEM((tm, 2 * D_GLA), F32),
            pltpu.VMEM((tm + 2 * SUBLANE, D_QK), F32),
            pltpu.VMEM((tm // CHUNK * SUBLANE, D_QK), F32),
            pltpu.VMEM((N_LEVELS + 2, tm, D_QK), BF16),
            pltpu.VMEM((N_LEVELS + 2, tm, D_QK), BF16),
            pltpu.VMEM((tm, D_GLA), F32),
            pltpu.VMEM((tm, D_MODEL), BF16),
            pltpu.VMEM((tm, D_MODEL), F32),
            pltpu.VMEM((tm, D_QK), F32),
        ],
        compiler_params=pltpu.CompilerParams(
            dimension_semantics=("arbitrary",), vmem_limit_bytes=VMEM_LIMIT),
        name="mix",
    )(x, nw, w_in, w_z, conv_w, gw2, gate_b, gla_norm, tri, w_out)


def kernel(x, ffn1_norm, ffn1_w_gate, ffn1_w_up, ffn1_w_down, mix_norm, w_in, conv_w, gate_w2,
           gate_b, gla_norm, w_out, ffn2_norm, ffn2_w_gate, ffn2_w_up, ffn2_w_down, final_norm):
    b, t, d = x.shape
    depth = w_in.shape[0]
    xs = x.reshape(b * t, d)
    tri = jnp.asarray(_chunk_tril(MIX_TM), BF16)
    fw = final_norm.reshape(1, d)
    ffn1 = (ffn1_w_gate, ffn1_w_up, ffn1_w_down)
    ffn2 = (ffn2_w_gate, ffn2_w_up, ffn2_w_down)
    weights = [w[0].astype(BF16) for w in ffn1]
    w_in_t = jnp.swapaxes(w_in, 1, 2)
    for l in range(depth):
        last = l == depth - 1
        xs, weights, (w_in_bf, w_out_bf, w_tail_t) = _ffn(xs, ffn1_norm[l].reshape(1, d), *weights, fw, final=False,
                                                cast=(ffn2, l), mix_cast=(w_in_t, w_out, l))
        gw2 = jnp.pad(gate_w2[l], ((0, LANE - GATE_RANK), (0, 0))).astype(BF16)
        xs = _mix(xs, mix_norm[l].reshape(1, d), w_in_bf, w_tail_t, conv_w[l], gw2,
                  gate_b[l].reshape(1, D_QK), gla_norm[l].reshape(1, HEAD_V), tri, w_out_bf, tm=MIX_TM)
        xs, weights, _ = _ffn(xs, ffn2_norm[l].reshape(1, d), *weights, fw, final=last,
                              cast=None if last else (ffn1, l + 1))
    return xs.reshape(b, t, d)
```
